```python
import math
import jax, jax.numpy as jnp
from jax import lax
import numpy as np

D_MODEL = 1024
BATCH = 4
SEQ = 8192
DEPTH = 2

N_MIXERS = 2
N_META = 16
GDN_HEADS = 8
GDN_HEAD_DIM = 128
GDN_WIDTH = GDN_HEADS * GDN_HEAD_DIM
GDN_CONV = 4
CHUNK = 64
META_PAD = (-N_META) % CHUNK
SC_WIDTH = D_MODEL
SC_CONV = 3
D_FF = 2816
FFN_CONV = 3
N_LAYERS_A = (DEPTH + 1) // 2
N_LAYERS_B = DEPTH // 2
ALPHA = (2.0 * DEPTH) ** 0.25
BETA_INIT = (8.0 * DEPTH) ** -0.25
LN_EPS = 1e-5
RMS_EPS = 1e-6
L2_EPS = 1e-6

kernel_name = "hybrid_gdn_shortconv_convffn_deepnorm"


def causal_dwconv(x, w):
    width, ch = w.shape
    return lax.conv_general_dilated(
        x, w[:, None, :].astype(x.dtype), window_strides=(1,), padding=[(width - 1, 0)],
        dimension_numbers=("NWC", "WIO", "NWC"), feature_group_count=ch)


def layer_norm(x, g, b):
    xf = x.astype(jnp.float32)
    mu = jnp.mean(xf, axis=-1, keepdims=True)
    var = jnp.mean(jnp.square(xf - mu), axis=-1, keepdims=True)
    y = (xf - mu) * lax.rsqrt(var + LN_EPS) * g.astype(jnp.float32) + b.astype(jnp.float32)
    return y.astype(x.dtype)


def l2norm(x):
    return x * lax.rsqrt(jnp.sum(x * x, axis=-1, keepdims=True) + L2_EPS)


def chunk_gated_delta_rule(q, k, v, g, beta):
    bsz, t_len, h, dk = q.shape
    n = t_len // CHUNK

    def to_chunks(t):
        return jnp.transpose(t.reshape(bsz, n, CHUNK, h, -1), (0, 3, 1, 2, 4))

    q, k, v = to_chunks(q), to_chunks(k), to_chunks(v)
    g = jnp.transpose(g.reshape(bsz, n, CHUNK, h), (0, 3, 1, 2))
    beta = jnp.transpose(beta.reshape(bsz, n, CHUNK, h), (0, 3, 1, 2))
    g = jnp.cumsum(g, axis=-1)

    causal = jnp.tril(jnp.ones((CHUNK, CHUNK), dtype=bool))
    strict = jnp.tril(jnp.ones((CHUNK, CHUNK), dtype=bool), -1)
    decay = jnp.exp(jnp.where(causal, g[..., :, None] - g[..., None, :], -jnp.inf))

    k_beta = k * beta[..., None]
    v_beta = v * beta[..., None]
    m = jnp.where(strict, jnp.einsum("bhnid,bhnjd->bhnij", k_beta, k) * decay, 0.0)
    a_mat = jnp.eye(CHUNK, dtype=q.dtype) + m
    u = lax.linalg.triangular_solve(a_mat, v_beta, left_side=True, lower=True)
    w = lax.linalg.triangular_solve(a_mat, k_beta * jnp.exp(g)[..., None], left_side=True, lower=True)

    qk = jnp.einsum("bhnid,bhnjd->bhnij", q, k) * decay
    q_g = q * jnp.exp(g)[..., None]
    g_last = g[..., -1]
    k_dec = k * jnp.exp(g_last[..., None] - g)[..., None]

    def step(state, inp):
        qg_i, kd_i, u_i, w_i, qk_i, gl_i = inp
        v_new = u_i - jnp.einsum("bhcd,bhde->bhce", w_i, state)
        o_i = jnp.einsum("bhcd,bhde->bhce", qg_i, state) + jnp.einsum("bhij,bhje->bhie", qk_i, v_new)
        state = state * jnp.exp(gl_i)[..., None, None] + jnp.einsum("bhcd,bhce->bhde", kd_i, v_new)
        return state, o_i

    xs = tuple(jnp.moveaxis(t, 2, 0) for t in (q_g, k_dec, u, w, qk, g_last))
    state0 = jnp.zeros((bsz, h, dk, v.shape[-1]), dtype=q.dtype)
    _, o = lax.scan(step, state0, xs)
    return jnp.transpose(o, (1, 0, 3, 2, 4)).reshape(bsz, t_len, h, -1)


def gated_deltanet(h, w_in, conv_w, a_log, dt_bias, norm_w, w_out):
    bsz, seq_len, _ = h.shape
    proj = h @ w_in
    qkv, z, b_raw, a_raw = jnp.split(
        proj, [3 * GDN_WIDTH, 4 * GDN_WIDTH, 4 * GDN_WIDTH + GDN_HEADS], axis=-1)
    qkv = jax.nn.silu(causal_dwconv(qkv, conv_w))
    q, k, v = [t.reshape(bsz, seq_len, GDN_HEADS, GDN_HEAD_DIM).astype(jnp.float32)
               for t in jnp.split(qkv, 3, axis=-1)]
    q = l2norm(q) * (GDN_HEAD_DIM ** -0.5)
    k = l2norm(k)
    beta = jax.nn.sigmoid(b_raw.astype(jnp.float32))
    g = -jnp.exp(a_log.astype(jnp.float32)) * jax.nn.softplus(
        a_raw.astype(jnp.float32) + dt_bias.astype(jnp.float32))
    pad4 = ((0, 0), (META_PAD, 0), (0, 0), (0, 0))
    pad3 = ((0, 0), (META_PAD, 0), (0, 0))
    o = chunk_gated_delta_rule(jnp.pad(q, pad4), jnp.pad(k, pad4), jnp.pad(v, pad4),
                               jnp.pad(g, pad3), jnp.pad(beta, pad3))[:, META_PAD:]
    o = o * lax.rsqrt(jnp.mean(o * o, axis=-1, keepdims=True) + RMS_EPS) * norm_w.astype(jnp.float32)
    o = o * jax.nn.silu(z.reshape(bsz, seq_len, GDN_HEADS, GDN_HEAD_DIM).astype(jnp.float32))
    return o.reshape(bsz, seq_len, GDN_WIDTH).astype(h.dtype) @ w_out


def short_conv_mixer(h, w_in, conv_w, w_out):
    b_gate, c_gate, xv = jnp.split(h @ w_in, 3, axis=-1)
    u = causal_dwconv(c_gate * xv, conv_w)
    return (b_gate * u) @ w_out


def conv_ffn(h, w_up, conv_w, w_down):
    u, gate = jnp.split(h @ w_up, 2, axis=-1)
    u = causal_dwconv(u, conv_w)
    return (jax.nn.silu(u) * gate) @ w_down


def setup_inputs(seed: int = 0) -> dict:
    key = jax.random.key(seed)
    ks = iter(jax.random.split(key, 32))
    f32 = jnp.float32

    def nrm(shape, scale):
        return jax.random.normal(next(ks), shape, f32) * scale

    a_in_cols = 4 * GDN_WIDTH + 2 * GDN_HEADS
    dt = jnp.exp(jax.random.uniform(next(ks), (N_LAYERS_A, GDN_HEADS), f32,
                                    math.log(1e-3), math.log(1e-1)))
    return {
        "x": nrm((BATCH, SEQ, D_MODEL), 1.0),
        "meta": nrm((N_META, D_MODEL), 1.0),
        "a_w_in": nrm((N_LAYERS_A, D_MODEL, a_in_cols), D_MODEL ** -0.5),
        "a_conv": nrm((N_LAYERS_A, GDN_CONV, 3 * GDN_WIDTH), GDN_CONV ** -0.5),
        "a_log": jnp.log(jax.random.uniform(next(ks), (N_LAYERS_A, GDN_HEADS), f32, 1.0, 16.0)),
        "a_dt_bias": dt + jnp.log(-jnp.expm1(-dt)),
        "a_norm": 1.0 + nrm((N_LAYERS_A, GDN_HEAD_DIM), 0.02),
        "a_w_out": nrm((N_LAYERS_A, GDN_WIDTH, D_MODEL), BETA_INIT * GDN_WIDTH ** -0.5),
        "b_w_in": nrm((N_LAYERS_B, D_MODEL, 3 * SC_WIDTH), D_MODEL ** -0.5),
        "b_conv": nrm((N_LAYERS_B, SC_CONV, SC_WIDTH), SC_CONV ** -0.5),
        "b_w_out": nrm((N_LAYERS_B, SC_WIDTH, D_MODEL), BETA_INIT * SC_WIDTH ** -0.5),
        "ln_mix_g": 1.0 + nrm((DEPTH, D_MODEL), 0.02),
        "ln_mix_b": nrm((DEPTH, D_MODEL), 0.02),
        "ffn_w_up": nrm((DEPTH, D_MODEL, 2 * D_FF), D_MODEL ** -0.5),
        "ffn_conv": nrm((DEPTH, FFN_CONV, D_FF), FFN_CONV ** -0.5),
        "ffn_w_down": nrm((DEPTH, D_FF, D_MODEL), BETA_INIT * D_FF ** -0.5),
        "ln_ffn_g": 1.0 + nrm((DEPTH, D_MODEL), 0.02),
        "ln_ffn_b": nrm((DEPTH, D_MODEL), 0.02),
    }


def reference(x, meta, a_w_in, a_conv, a_log, a_dt_bias, a_norm, a_w_out,
              b_w_in, b_conv, b_w_out, ln_mix_g, ln_mix_b,
              ffn_w_up, ffn_conv, ffn_w_down, ln_ffn_g, ln_ffn_b):
    bsz = x.shape[0]
    h = jnp.concatenate(
        [jnp.broadcast_to(meta.astype(x.dtype)[None], (bsz, N_META, D_MODEL)), x], axis=1)
    for i in range(DEPTH):
        j = i // N_MIXERS
        if i % N_MIXERS == 0:
            mix = gated_deltanet(h, a_w_in[j], a_conv[j], a_log[j], a_dt_bias[j], a_norm[j], a_w_out[j])
        else:
            mix = short_conv_mixer(h, b_w_in[j], b_conv[j], b_w_out[j])
        h = layer_norm(ALPHA * h + mix, ln_mix_g[i], ln_mix_b[i])
        h = layer_norm(ALPHA * h + conv_ffn(h, ffn_w_up[i], ffn_conv[i], ffn_w_down[i]),
                       ln_ffn_g[i], ln_ffn_b[i])
    return h[:, N_META:]
```

```python
import functools

import jax
import jax.numpy as jnp
from jax import lax
from jax.experimental import pallas as pl
from jax.experimental.pallas import tpu as pltpu

F32 = jnp.float32
BF16 = jnp.bfloat16

N_META = 16
GDN_HEADS = 8
GDN_HEAD_DIM = 128
GDN_WIDTH = GDN_HEADS * GDN_HEAD_DIM
CHUNK = 64
INV_BLOCK = 16
HALO = 8
LANES = 128
LN_EPS = 1e-5
RMS_EPS = 1e-6
L2_EPS = 1e-6
MASK_NEG = -1e30

ROW_TILE = 512
GDN_ROW_TILE = 256
QKV_COL_CHUNK = 512
FFN_COL_CHUNK = 256
VMEM_LIMIT = 56 * 1024 * 1024


def _dot(a, b):
    return jnp.dot(a, b, preferred_element_type=F32)


def _dot_nt(a, b):
    return lax.dot_general(a, b, (((1,), (1,)), ((), ())), preferred_element_type=F32)


def _sigmoid(x):
    return 1.0 / (1.0 + jnp.exp(-x))


def _layer_norm(x, g, b):
    mu = jnp.mean(x, axis=-1, keepdims=True)
    xc = x - mu
    var = jnp.mean(xc * xc, axis=-1, keepdims=True)
    return xc * lax.rsqrt(var + LN_EPS) * g + b


def _const_spec(shape):
    nd = len(shape)
    return pl.BlockSpec(shape, lambda *_, _nd=nd: (0,) * _nd, pipeline_mode=pl.Buffered(1))


def _row_spec(tm, width):
    return pl.BlockSpec((1, tm, width), lambda b, t: (b, t, 0))


def _tail_spec(width):
    return pl.BlockSpec((1, HALO, width), lambda b, t: (b, 0, 0))


def _params():
    return pltpu.CompilerParams(dimension_semantics=("arbitrary", "arbitrary"),
                                vmem_limit_bytes=VMEM_LIMIT)


def _causal_taps(st_ref, carry, cur, taps, tm):
    width = taps.shape[0]
    st_ref[0:HALO, :] = carry
    st_ref[HALO:HALO + tm, :] = cur
    y = cur * taps[width - 1:width, :]
    for j in range(width - 1):
        back = width - 1 - j
        y = y + st_ref[HALO - back:HALO - back + tm, :] * taps[j:j + 1, :]
    return y


def _gdn_in_kernel(h_ref, wqkv_ref, wz_ref, wba_ref, cw_ref, gs_ref, dtb_ref, halo_ref,
                   q_ref, k_ref, v_ref, z_ref, sc_ref, tail_ref, st_ref, carry_ref, *, tm):
    t = pl.program_id(1)

    @pl.when(t == 0)
    def _():
        carry_ref[...] = halo_ref[...]

    hb = h_ref[0].astype(BF16)
    z_ref[0] = _dot(hb, wz_ref[...]).astype(z_ref.dtype)

    ba = _dot(hb, wba_ref[...])
    lane = lax.broadcasted_iota(jnp.int32, ba.shape, 1)
    xg = ba + dtb_ref[...]
    softplus = jnp.maximum(xg, 0.0) + jnp.log(1.0 + jnp.exp(-jnp.abs(xg)))
    sc_ref[0] = jnp.where(lane < GDN_HEADS, _sigmoid(ba), gs_ref[...] * softplus)

    outs = (q_ref, k_ref, v_ref)
    heads_per_chunk = QKV_COL_CHUNK // GDN_HEAD_DIM
    chunks_per_out = GDN_WIDTH // QKV_COL_CHUNK
    for c in range(3 * chunks_per_out):
        cs = slice(c * QKV_COL_CHUNK, (c + 1) * QKV_COL_CHUNK)
        p = _dot(hb, wqkv_ref[:, cs])
        y = _causal_taps(st_ref, carry_ref[:, cs], p, cw_ref[:, cs], tm)
        carry_ref[:, cs] = p[tm - HALO:tm, :]
        y = y * _sigmoid(y)
        which = c // chunks_per_out
        for hh in range(heads_per_chunk):
            yh = y[:, hh * GDN_HEAD_DIM:(hh + 1) * GDN_HEAD_DIM]
            if which < 2:
                yh = yh * lax.rsqrt(jnp.sum(yh * yh, axis=-1, keepdims=True) + L2_EPS)
            if which == 0:
                yh = yh * (GDN_HEAD_DIM ** -0.5)
            col = (c % chunks_per_out) * QKV_COL_CHUNK + hh * GDN_HEAD_DIM
            outs[which][0, :, col:col + GDN_HEAD_DIM] = yh.astype(outs[which].dtype)

    @pl.when(t == pl.num_programs(1) - 1)
    def _():
        tail_ref[0] = carry_ref[...]


def _gdn_in(h, wqkv, wz, wba, conv_w, gscale, dtb, halo, tm):
    bsz, t_len, d = h.shape
    w3 = wqkv.shape[1]
    row = lambda width: _row_spec(tm, width)
    out_shape = (
        jax.ShapeDtypeStruct((bsz, t_len, GDN_WIDTH), F32),
        jax.ShapeDtypeStruct((bsz, t_len, GDN_WIDTH), F32),
        jax.ShapeDtypeStruct((bsz, t_len, GDN_WIDTH), F32),
        jax.ShapeDtypeStruct((bsz, t_len, GDN_WIDTH), F32),
        jax.ShapeDtypeStruct((bsz, t_len, LANES), F32),
        jax.ShapeDtypeStruct((bsz, HALO, w3), F32),
    )
    return pl.pallas_call(
        functools.partial(_gdn_in_kernel, tm=tm),
        grid=(bsz, t_len // tm),
        in_specs=[row(d), _const_spec(wqkv.shape), _const_spec(wz.shape), _const_spec(wba.shape),
                  _const_spec(conv_w.shape), _const_spec(gscale.shape), _const_spec(dtb.shape),
                  _const_spec(halo.shape)],
        out_specs=(row(GDN_WIDTH), row(GDN_WIDTH), row(GDN_WIDTH), row(GDN_WIDTH), row(LANES),
                   _tail_spec(w3)),
        out_shape=out_shape,
        scratch_shapes=[pltpu.VMEM((tm + HALO, QKV_COL_CHUNK), F32), pltpu.VMEM((HALO, w3), F32)],
        compiler_params=_params(),
        name="gdn_in",
    )(h, wqkv, wz, wba, conv_w, gscale, dtb, halo)


def _unit_lower_inverse(m, eye, blockdiag):
    mm = lambda a, b: _dot(a.astype(BF16), b.astype(BF16))
    d = jnp.where(blockdiag, m, 0.0)
    lo = jnp.where(blockdiag, 0.0, m)
    d2 = mm(d, d)
    d4 = mm(d2, d2)
    d8 = mm(d4, d4)
    p = mm(eye - d, eye + d2)
    p = mm(p, eye + d4)
    p = mm(p, eye + d8)
    n = mm(p, lo)
    n2 = mm(n, n)
    return mm(mm(eye - n, eye + n2), p)


def _gdn_kernel(q_ref, k_ref, v_ref, z_ref, sc_ref, nw_ref, s0_ref, o_ref, sfin_ref, state_ref,
                *, n_chunks):
    t = pl.program_id(1)

    @pl.when(t == 0)
    def _():
        state_ref[...] = s0_ref[...]

    ri = lax.broadcasted_iota(jnp.int32, (CHUNK, CHUNK), 0)
    ci = lax.broadcasted_iota(jnp.int32, (CHUNK, CHUNK), 1)
    causal = ri >= ci
    strict = ri > ci
    blockdiag = (ri // INV_BLOCK) == (ci // INV_BLOCK)
    eye = jnp.where(ri == ci, 1.0, 0.0).astype(F32)
    tril_b = jnp.where(causal, 1.0, 0.0).astype(BF16)
    nw = nw_ref[...]

    def chunk_body(c, carry):
        r0 = pl.multiple_of(c * CHUNK, CHUNK)
        rows = pl.ds(r0, CHUNK)
        scc = sc_ref[0, rows, :]
        hi = scc.astype(BF16)
        r1 = scc - hi.astype(F32)
        mid = r1.astype(BF16)
        low = (r1 - mid.astype(F32)).astype(BF16)
        gc = _dot(tril_b, hi) + _dot(tril_b, mid) + _dot(tril_b, low)
        gct = gc.T
        for h in range(GDN_HEADS):
            cs = slice(h * GDN_HEAD_DIM, (h + 1) * GDN_HEAD_DIM)
            qh = q_ref[0, rows, cs].astype(F32)
            kh = k_ref[0, rows, cs].astype(F32)
            vh = v_ref[0, rows, cs].astype(F32)
            beta = scc[:, h:h + 1]
            gcol = gc[:, GDN_HEADS + h:GDN_HEADS + h + 1]
            grow = gct[GDN_HEADS + h:GDN_HEADS + h + 1, :]
            glast = gcol[CHUNK - 1:CHUNK, :]
            eg = jnp.exp(gcol)
            decay = jnp.exp(jnp.where(causal, gcol - grow, MASK_NEG))

            kb = kh * beta
            kk_qk = _dot_nt(jnp.concatenate([kb, qh], axis=0).astype(BF16), kh.astype(BF16))
            m = jnp.where(strict, kk_qk[:CHUNK] * decay, 0.0)
            qk = kk_qk[CHUNK:] * decay
            tinv = _unit_lower_inverse(m, eye, blockdiag)
            rhs = jnp.concatenate([vh * beta, kb * eg], axis=1).astype(BF16)
            uw = _dot(tinv.astype(BF16), rhs)
            u = uw[:, :GDN_HEAD_DIM]
            w = uw[:, GDN_HEAD_DIM:]

            s = state_ref[h]
            ws_qs = _dot(jnp.concatenate([w, qh * eg], axis=0).astype(BF16), s.astype(BF16))
            v_new = u - ws_qs[:CHUNK]
            kd = kh * jnp.exp(glast - gcol)
            lhs2 = jnp.concatenate([qk, kd.T], axis=0).astype(BF16)
            r2 = _dot(lhs2, v_new.astype(BF16))
            o = ws_qs[CHUNK:] + r2[:CHUNK]
            state_ref[h] = s * jnp.exp(glast) + r2[CHUNK:]

            o = o * lax.rsqrt(jnp.mean(o * o, axis=-1, keepdims=True) + RMS_EPS) * nw
            zh = z_ref[0, rows, cs].astype(F32)
            o_ref[0, rows, cs] = (o * (zh * _sigmoid(zh))).astype(o_ref.dtype)
        return carry

    lax.fori_loop(0, n_chunks, chunk_body, 0)

    @pl.when(t == pl.num_programs(1) - 1)
    def _():
        sfin_ref[0] = state_ref[...]


def _gdn(q, k, v, z, sc, norm_w, state0, tm):
    bsz, t_len, _ = q.shape
    row = lambda width: _row_spec(tm, width)
    state_shape = (GDN_HEADS, GDN_HEAD_DIM, GDN_HEAD_DIM)
    return pl.pallas_call(
        functools.partial(_gdn_kernel, n_chunks=tm // CHUNK),
        grid=(bsz, t_len // tm),
        in_specs=[row(GDN_WIDTH), row(GDN_WIDTH), row(GDN_WIDTH), row(GDN_WIDTH), row(LANES),
                  _const_spec(norm_w.shape), _const_spec(state0.shape)],
        out_specs=(row(GDN_WIDTH),
                   pl.BlockSpec((1,) + state_shape, lambda b, t: (b, 0, 0, 0))),
        out_shape=(jax.ShapeDtypeStruct((bsz, t_len, GDN_WIDTH), BF16),
                   jax.ShapeDtypeStruct((bsz,) + state_shape, F32)),
        scratch_shapes=[pltpu.VMEM(state_shape, F32)],
        compiler_params=_params(),
        name="gdn_scan",
    )(q, k, v, z, sc, norm_w, state0)


def _proj_ln_kernel(a_ref, h_ref, w_ref, g_ref, b_ref, o_ref, *, alpha):
    y = _dot(a_ref[0], w_ref[...])
    o_ref[0] = _layer_norm(alpha * h_ref[0] + y, g_ref[...], b_ref[...])


def _proj_ln(a, h, w, g, b, alpha, tm):
    bsz, t_len, d = h.shape
    return pl.pallas_call(
        functools.partial(_proj_ln_kernel, alpha=alpha),
        grid=(bsz, t_len // tm),
        in_specs=[_row_spec(tm, a.shape[2]), _row_spec(tm, d), _const_spec(w.shape),
                  _const_spec(g.shape), _const_spec(b.shape)],
        out_specs=_row_spec(tm, d),
        out_shape=jax.ShapeDtypeStruct((bsz, t_len, d), F32),
        compiler_params=_params(),
        name="proj_ln",
    )(a, h, w, g, b)


def _sconv_kernel(h_ref, win_ref, cw_ref, wout_ref, g_ref, b_ref, halo_ref, o_ref, tail_ref,
                  st_ref, carry_ref, *, alpha, tm):
    t = pl.program_id(1)

    @pl.when(t == 0)
    def _():
        carry_ref[...] = halo_ref[...]

    hv = h_ref[0]
    hb = hv.astype(BF16)
    width = cw_ref.shape[1]
    b_gate = _dot(hb, win_ref[:, 0:width])
    cx = _dot(hb, win_ref[:, width:2 * width]) * _dot(hb, win_ref[:, 2 * width:3 * width])
    u = _causal_taps(st_ref, carry_ref[...], cx, cw_ref[...], tm)
    carry_ref[...] = cx[tm - HALO:tm, :]
    y = _dot((b_gate * u).astype(BF16), wout_ref[...])
    o_ref[0] = _layer_norm(alpha * hv + y, g_ref[...], b_ref[...])

    @pl.when(t == pl.num_programs(1) - 1)
    def _():
        tail_ref[0] = carry_ref[...]


def _sconv(h, w_in, conv_w, w_out, g, b, halo, alpha, tm):
    bsz, t_len, d = h.shape
    width = conv_w.shape[1]
    return pl.pallas_call(
        functools.partial(_sconv_kernel, alpha=alpha, tm=tm),
        grid=(bsz, t_len // tm),
        in_specs=[_row_spec(tm, d), _const_spec(w_in.shape), _const_spec(conv_w.shape),
                  _const_spec(w_out.shape), _const_spec(g.shape), _const_spec(b.shape),
                  _const_spec(halo.shape)],
        out_specs=(_row_spec(tm, d), _tail_spec(width)),
        out_shape=(jax.ShapeDtypeStruct((bsz, t_len, d), F32),
                   jax.ShapeDtypeStruct((bsz, HALO, width), F32)),
        scratch_shapes=[pltpu.VMEM((tm + HALO, width), F32), pltpu.VMEM((HALO, width), F32)],
        compiler_params=_params(),
        name="sconv_mixer",
    )(h, w_in, conv_w, w_out, g, b, halo)


def _ffn_kernel(h_ref, wu_ref, wg_ref, cw_ref, wd_ref, g_ref, b_ref, halo_ref, o_ref, tail_ref,
                st_ref, carry_ref, acc_ref, *, alpha, tm):
    t = pl.program_id(1)

    @pl.when(t == 0)
    def _():
        carry_ref[...] = halo_ref[...]

    hv = h_ref[0]
    hb = hv.astype(BF16)
    d_ff = cw_ref.shape[1]
    for c in range(d_ff // FFN_COL_CHUNK):
        cs = slice(c * FFN_COL_CHUNK, (c + 1) * FFN_COL_CHUNK)
        u = _dot(hb, wu_ref[:, cs])
        gate = _dot(hb, wg_ref[:, cs])
        uc = _causal_taps(st_ref, carry_ref[:, cs], u, cw_ref[:, cs], tm)
        carry_ref[:, cs] = u[tm - HALO:tm, :]
        act = (uc * _sigmoid(uc) * gate).astype(BF16)
        part = _dot(act, wd_ref[cs, :])
        if c == 0:
            acc_ref[...] = part
        else:
            acc_ref[...] += part
    o_ref[0] = _layer_norm(alpha * hv + acc_ref[...], g_ref[...], b_ref[...])

    @pl.when(t == pl.num_programs(1) - 1)
    def _():
        tail_ref[0] = carry_ref[...]


def _ffn(h, w_u, w_g, conv_w, w_down, g, b, halo, alpha, tm):
    bsz, t_len, d = h.shape
    d_ff = conv_w.shape[1]
    return pl.pallas_call(
        functools.partial(_ffn_kernel, alpha=alpha, tm=tm),
        grid=(bsz, t_len // tm),
        in_specs=[_row_spec(tm, d), _const_spec(w_u.shape), _const_spec(w_g.shape),
                  _const_spec(conv_w.shape), _const_spec(w_down.shape), _const_spec(g.shape),
                  _const_spec(b.shape), _const_spec(halo.shape)],
        out_specs=(_row_spec(tm, d), _tail_spec(d_ff)),
        out_shape=(jax.ShapeDtypeStruct((bsz, t_len, d), F32),
                   jax.ShapeDtypeStruct((bsz, HALO, d_ff), F32)),
        scratch_shapes=[pltpu.VMEM((tm + HALO, FFN_COL_CHUNK), F32), pltpu.VMEM((HALO, d_ff), F32),
                        pltpu.VMEM((tm, d), F32)],
        compiler_params=_params(),
        name="conv_ffn",
    )(h, w_u, w_g, conv_w, w_down, g, b, halo)


def _gdn_layer(hm, hx, w_in, conv_w, a_log, dt_bias, norm_w, w_out, ln_g, ln_b, alpha):
    w3 = 3 * GDN_WIDTH
    wqkv = w_in[:, :w3].astype(BF16)
    wz = w_in[:, w3:w3 + GDN_WIDTH].astype(BF16)
    wba = jnp.pad(w_in[:, w3 + GDN_WIDTH:], ((0, 0), (0, LANES - 2 * GDN_HEADS))).astype(BF16)
    lane_pad = (GDN_HEADS, LANES - 2 * GDN_HEADS)
    gscale = jnp.pad(-jnp.exp(a_log.astype(F32)), lane_pad)[None]
    dtb = jnp.pad(dt_bias.astype(F32), lane_pad)[None]
    nw = norm_w.astype(F32)[None]
    w_out_b = w_out.astype(BF16)
    g2, b2 = ln_g[None], ln_b[None]
    state_shape = (GDN_HEADS, GDN_HEAD_DIM, GDN_HEAD_DIM)

    qm, km, vm, zm, scm, tail_m = _gdn_in(hm, wqkv, wz, wba, conv_w, gscale, dtb,
                                          jnp.zeros((HALO, w3), F32), N_META)
    front = ((0, 0), (CHUNK - N_META, 0), (0, 0))
    om, state_m = _gdn(jnp.pad(qm, front), jnp.pad(km, front), jnp.pad(vm, front),
                       jnp.pad(zm, front), jnp.pad(scm, front), nw,
                       jnp.zeros(state_shape, F32), CHUNK)
    hm_new = _proj_ln(om[:, CHUNK - N_META:], hm, w_out_b, g2, b2, alpha, N_META)

    qx, kx, vx, zx, scx, _ = _gdn_in(hx, wqkv, wz, wba, conv_w, gscale, dtb, tail_m[0], ROW_TILE)
    ox, _ = _gdn(qx, kx, vx, zx, scx, nw, state_m[0], GDN_ROW_TILE)
    hx_new = _proj_ln(ox, hx, w_out_b, g2, b2, alpha, ROW_TILE)
    return hm_new, hx_new


def _sconv_layer(hm, hx, w_in, conv_w, w_out, ln_g, ln_b, alpha):
    w_in_b = w_in.astype(BF16)
    w_out_b = w_out.astype(BF16)
    g2, b2 = ln_g[None], ln_b[None]
    width = conv_w.shape[1]
    hm_new, tail_m = _sconv(hm, w_in_b, conv_w, w_out_b, g2, b2, jnp.zeros((HALO, width), F32),
                            alpha, N_META)
    hx_new, _ = _sconv(hx, w_in_b, conv_w, w_out_b, g2, b2, tail_m[0], alpha, ROW_TILE)
    return hm_new, hx_new


def _ffn_layer(hm, hx, w_up, conv_w, w_down, ln_g, ln_b, alpha):
    d_ff = conv_w.shape[1]
    w_u = w_up[:, :d_ff].astype(BF16)
    w_g = w_up[:, d_ff:].astype(BF16)
    w_d = w_down.astype(BF16)
    g2, b2 = ln_g[None], ln_b[None]
    hm_new, tail_m = _ffn(hm, w_u, w_g, conv_w, w_d, g2, b2, jnp.zeros((HALO, d_ff), F32),
                          alpha, N_META)
    hx_new, _ = _ffn(hx, w_u, w_g, conv_w, w_d, g2, b2, tail_m[0], alpha, ROW_TILE)
    return hm_new, hx_new


def kernel(x, meta, a_w_in, a_conv, a_log, a_dt_bias, a_norm, a_w_out, b_w_in, b_conv, b_w_out,
           ln_mix_g, ln_mix_b, ffn_w_up, ffn_conv, ffn_w_down, ln_ffn_g, ln_ffn_b):
    depth = ln_mix_g.shape[0]
    alpha = (2.0 * depth) ** 0.25
    assert x.shape[1] % ROW_TILE == 0 and meta.shape[0] == N_META
    hx = x
    hm = meta.astype(x.dtype)[None]
    for i in range(depth):
        j = i // 2
        if i % 2 == 0:
            hm, hx = _gdn_layer(hm, hx, a_w_in[j], a_conv[j], a_log[j], a_dt_bias[j], a_norm[j],
                                a_w_out[j], ln_mix_g[i], ln_mix_b[i], alpha)
        else:
            hm, hx = _sconv_layer(hm, hx, b_w_in[j], b_conv[j], b_w_out[j], ln_mix_g[i],
                                  ln_mix_b[i], alpha)
        hm, hx = _ffn_layer(hm, hx, ffn_w_up[i], ffn_conv[i], ffn_w_down[i], ln_ffn_g[i],
                            ln_ffn_b[i], alpha)
    return hx
```

```python
import functools

import jax
import jax.numpy as jnp
from jax import lax
from jax.experimental import pallas as pl
from jax.experimental.pallas import tpu as pltpu

F32 = jnp.float32
BF16 = jnp.bfloat16

N_META = 16
GDN_HEADS = 8
GDN_HEAD_DIM = 128
GDN_WIDTH = GDN_HEADS * GDN_HEAD_DIM
CHUNK = 64
INV_BLOCK = 16
HALO = 8
LANES = 128
LN_EPS = 1e-5
RMS_EPS = 1e-6
L2_EPS = 1e-6
MASK_NEG = -1e30

ROW_TILE = 512
GDN_ROW_TILE = 256
QKV_COL_CHUNK = 512
FFN_COL_CHUNK = 256
VMEM_LIMIT = 56 * 1024 * 1024


def _dot(a, b):
    return jnp.dot(a, b, preferred_element_type=F32)


def _dot_nt(a, b):
    return lax.dot_general(a, b, (((1,), (1,)), ((), ())), preferred_element_type=F32)


def _sigmoid(x):
    return 1.0 / (1.0 + jnp.exp(-x))


def _layer_norm(x, g, b):
    mu = jnp.mean(x, axis=-1, keepdims=True)
    xc = x - mu
    var = jnp.mean(xc * xc, axis=-1, keepdims=True)
    return xc * lax.rsqrt(var + LN_EPS) * g + b


def _const_spec(shape):
    nd = len(shape)
    return pl.BlockSpec(shape, lambda *_, _nd=nd: (0,) * _nd, pipeline_mode=pl.Buffered(1))


def _row_spec(tm, width):
    return pl.BlockSpec((1, tm, width), lambda b, t: (b, t, 0))


def _tail_spec(width):
    return pl.BlockSpec((1, HALO, width), lambda b, t: (b, 0, 0))


def _params():
    return pltpu.CompilerParams(dimension_semantics=("arbitrary", "arbitrary"),
                                vmem_limit_bytes=VMEM_LIMIT)


def _causal_taps(st_ref, carry, cur, taps, tm):
    width = taps.shape[0]
    st_ref[0:HALO, :] = carry
    st_ref[HALO:HALO + tm, :] = cur
    y = cur * taps[width - 1:width, :]
    for j in range(width - 1):
        back = width - 1 - j
        y = y + st_ref[HALO - back:HALO - back + tm, :] * taps[j:j + 1, :]
    return y


def _gdn_in_kernel(h_ref, wqkv_ref, wz_ref, wba_ref, cw_ref, gs_ref, dtb_ref, halo_ref,
                   q_ref, k_ref, v_ref, z_ref, sc_ref, tail_ref, st_ref, carry_ref, *, tm):
    t = pl.program_id(1)

    @pl.when(t == 0)
    def _():
        carry_ref[...] = halo_ref[...]

    hb = h_ref[0].astype(BF16)
    z_ref[0] = _dot(hb, wz_ref[...]).astype(z_ref.dtype)

    ba = _dot(hb, wba_ref[...])
    lane = lax.broadcasted_iota(jnp.int32, ba.shape, 1)
    xg = ba + dtb_ref[...]
    softplus = jnp.maximum(xg, 0.0) + jnp.log(1.0 + jnp.exp(-jnp.abs(xg)))
    sc_ref[0] = jnp.where(lane < GDN_HEADS, _sigmoid(ba), gs_ref[...] * softplus)

    outs = (q_ref, k_ref, v_ref)
    heads_per_chunk = QKV_COL_CHUNK // GDN_HEAD_DIM
    chunks_per_out = GDN_WIDTH // QKV_COL_CHUNK
    for c in range(3 * chunks_per_out):
        cs = slice(c * QKV_COL_CHUNK, (c + 1) * QKV_COL_CHUNK)
        p = _dot(hb, wqkv_ref[:, cs])
        y = _causal_taps(st_ref, carry_ref[:, cs], p, cw_ref[:, cs], tm)
        carry_ref[:, cs] = p[tm - HALO:tm, :]
        y = y * _sigmoid(y)
        which = c // chunks_per_out
        for hh in range(heads_per_chunk):
            yh = y[:, hh * GDN_HEAD_DIM:(hh + 1) * GDN_HEAD_DIM]
            if which < 2:
                yh = yh * lax.rsqrt(jnp.sum(yh * yh, axis=-1, keepdims=True) + L2_EPS)
            if which == 0:
                yh = yh * (GDN_HEAD_DIM ** -0.5)
            col = (c % chunks_per_out) * QKV_COL_CHUNK + hh * GDN_HEAD_DIM
            outs[which][0, :, col:col + GDN_HEAD_DIM] = yh.astype(outs[which].dtype)

    @pl.when(t == pl.num_programs(1) - 1)
    def _():
        tail_ref[0] = carry_ref[...]


def _gdn_in(h, wqkv, wz, wba, conv_w, gscale, dtb, halo, tm):
    bsz, t_len, d = h.shape
    w3 = wqkv.shape[1]
    row = lambda width: _row_spec(tm, width)
    out_shape = (
        jax.ShapeDtypeStruct((bsz, t_len, GDN_WIDTH), F32),
        jax.ShapeDtypeStruct((bsz, t_len, GDN_WIDTH), F32),
        jax.ShapeDtypeStruct((bsz, t_len, GDN_WIDTH), F32),
        jax.ShapeDtypeStruct((bsz, t_len, GDN_WIDTH), F32),
        jax.ShapeDtypeStruct((bsz, t_len, LANES), F32),
        jax.ShapeDtypeStruct((bsz, HALO, w3), F32),
    )
    return pl.pallas_call(
        functools.partial(_gdn_in_kernel, tm=tm),
        grid=(bsz, t_len // tm),
        in_specs=[row(d), _const_spec(wqkv.shape), _const_spec(wz.shape), _const_spec(wba.shape),
                  _const_spec(conv_w.shape), _const_spec(gscale.shape), _const_spec(dtb.shape),
                  _const_spec(halo.shape)],
        out_specs=(row(GDN_WIDTH), row(GDN_WIDTH), row(GDN_WIDTH), row(GDN_WIDTH), row(LANES),
                   _tail_spec(w3)),
        out_shape=out_shape,
        scratch_shapes=[pltpu.VMEM((tm + HALO, QKV_COL_CHUNK), F32), pltpu.VMEM((HALO, w3), F32)],
        compiler_params=_params(),
        name="gdn_in",
    )(h, wqkv, wz, wba, conv_w, gscale, dtb, halo)


def _mm_each(a_list, b_list):
    return [_dot(a.astype(BF16), b.astype(BF16)) for a, b in zip(a_list, b_list)]


def _unit_lower_inverse(ms, eye, blockdiag):
    d = [jnp.where(blockdiag, m, 0.0) for m in ms]
    lo = [jnp.where(blockdiag, 0.0, m) for m in ms]
    d2 = _mm_each(d, d)
    d4 = _mm_each(d2, d2)
    p = _mm_each([eye - x for x in d], [eye + x for x in d2])
    d8 = _mm_each(d4, d4)
    p = _mm_each(p, [eye + x for x in d4])
    p = _mm_each(p, [eye + x for x in d8])
    n = _mm_each(p, lo)
    n2 = _mm_each(n, n)
    qn = _mm_each([eye - x for x in n], [eye + x for x in n2])
    return _mm_each(qn, p)


def _gdn_kernel(q_ref, k_ref, v_ref, z_ref, sc_ref, nw_ref, s0_ref, o_ref, sfin_ref, state_ref,
                *, n_chunks):
    t = pl.program_id(1)

    @pl.when(t == 0)
    def _():
        state_ref[...] = s0_ref[...]

    ri = lax.broadcasted_iota(jnp.int32, (CHUNK, CHUNK), 0)
    ci = lax.broadcasted_iota(jnp.int32, (CHUNK, CHUNK), 1)
    causal = ri >= ci
    strict = ri > ci
    blockdiag = (ri // INV_BLOCK) == (ci // INV_BLOCK)
    eye = jnp.where(ri == ci, 1.0, 0.0).astype(F32)
    tril_b = jnp.where(causal, 1.0, 0.0).astype(BF16)
    nw = nw_ref[...]

    heads = range(GDN_HEADS)
    chunks = range(n_chunks)
    cols = [slice(h * GDN_HEAD_DIM, (h + 1) * GDN_HEAD_DIM) for h in heads]
    rows = [slice(c * CHUNK, (c + 1) * CHUNK) for c in chunks]
    items = [(c, h) for c in chunks for h in heads]

    scc = [sc_ref[0, r, :] for r in rows]
    hi = [x.astype(BF16) for x in scc]
    r1 = [x - y.astype(F32) for x, y in zip(scc, hi)]
    mid = [x.astype(BF16) for x in r1]
    low = [(x - y.astype(F32)).astype(BF16) for x, y in zip(r1, mid)]
    gc = [_dot(tril_b, a) + _dot(tril_b, b) + _dot(tril_b, c_)
          for a, b, c_ in zip(hi, mid, low)]
    gct = [x.T for x in gc]

    q = [q_ref[0, rows[c], cols[h]].astype(F32) for c, h in items]
    k = [k_ref[0, rows[c], cols[h]].astype(F32) for c, h in items]
    v = [v_ref[0, rows[c], cols[h]].astype(F32) for c, h in items]
    beta = [scc[c][:, h:h + 1] for c, h in items]
    gcol = [gc[c][:, GDN_HEADS + h:GDN_HEADS + h + 1] for c, h in items]
    grow = [gct[c][GDN_HEADS + h:GDN_HEADS + h + 1, :] for c, h in items]
    glast = [g[CHUNK - 1:CHUNK, :] for g in gcol]
    eg = [jnp.exp(g) for g in gcol]
    decay = [jnp.exp(jnp.where(causal, gc_ - gr_, MASK_NEG)) for gc_, gr_ in zip(gcol, grow)]
    kb = [k_ * b_ for k_, b_ in zip(k, beta)]
    kk_qk = [_dot_nt(jnp.concatenate([kb_, q_], axis=0).astype(BF16), k_.astype(BF16))
             for kb_, q_, k_ in zip(kb, q, k)]
    m = [jnp.where(strict, x[:CHUNK] * d_, 0.0) for x, d_ in zip(kk_qk, decay)]
    qk = [x[CHUNK:] * d_ for x, d_ in zip(kk_qk, decay)]
    tinv = _unit_lower_inverse(m, eye, blockdiag)
    rhs = [jnp.concatenate([v_ * b_, kb_ * e_], axis=1)
           for v_, b_, kb_, e_ in zip(v, beta, kb, eg)]
    uw = _mm_each(tinv, rhs)
    u = [x[:, :GDN_HEAD_DIM] for x in uw]
    lhs1 = [jnp.concatenate([x[:, GDN_HEAD_DIM:], q_ * e_], axis=0).astype(BF16)
            for x, q_, e_ in zip(uw, q, eg)]
    lhs2 = [jnp.concatenate([a, (k_ * jnp.exp(gl_ - gc_)).T], axis=0).astype(BF16)
            for a, k_, gl_, gc_ in zip(qk, k, glast, gcol)]
    sdec = [jnp.exp(g) for g in glast]

    for c in chunks:
        idx = [c * GDN_HEADS + h for h in heads]
        s = [state_ref[h] for h in heads]
        ws_qs = _mm_each([lhs1[i] for i in idx], s)
        v_new = [u[i] - y[:CHUNK] for i, y in zip(idx, ws_qs)]
        r2 = _mm_each([lhs2[i] for i in idx], v_new)
        for h in heads:
            state_ref[h] = s[h] * sdec[idx[h]] + r2[h][CHUNK:]
        for h in heads:
            o = ws_qs[h][CHUNK:] + r2[h][:CHUNK]
            o = o * lax.rsqrt(jnp.mean(o * o, axis=-1, keepdims=True) + RMS_EPS) * nw
            zh = z_ref[0, rows[c], cols[h]].astype(F32)
            o_ref[0, rows[c], cols[h]] = (o * (zh * _sigmoid(zh))).astype(o_ref.dtype)

    @pl.when(t == pl.num_programs(1) - 1)
    def _():
        sfin_ref[0] = state_ref[...]


def _gdn(q, k, v, z, sc, norm_w, state0, tm):
    bsz, t_len, _ = q.shape
    row = lambda width: _row_spec(tm, width)
    state_shape = (GDN_HEADS, GDN_HEAD_DIM, GDN_HEAD_DIM)
    return pl.pallas_call(
        functools.partial(_gdn_kernel, n_chunks=tm // CHUNK),
        grid=(bsz, t_len // tm),
        in_specs=[row(GDN_WIDTH), row(GDN_WIDTH), row(GDN_WIDTH), row(GDN_WIDTH), row(LANES),
                  _const_spec(norm_w.shape), _const_spec(state0.shape)],
        out_specs=(row(GDN_WIDTH),
                   pl.BlockSpec((1,) + state_shape, lambda b, t: (b, 0, 0, 0))),
        out_shape=(jax.ShapeDtypeStruct((bsz, t_len, GDN_WIDTH), BF16),
                   jax.ShapeDtypeStruct((bsz,) + state_shape, F32)),
        scratch_shapes=[pltpu.VMEM(state_shape, F32)],
        compiler_params=_params(),
        name="gdn_scan",
    )(q, k, v, z, sc, norm_w, state0)


def _proj_ln_kernel(a_ref, h_ref, w_ref, g_ref, b_ref, o_ref, *, alpha):
    y = _dot(a_ref[0], w_ref[...])
    o_ref[0] = _layer_norm(alpha * h_ref[0] + y, g_ref[...], b_ref[...])


def _proj_ln(a, h, w, g, b, alpha, tm):
    bsz, t_len, d = h.shape
    return pl.pallas_call(
        functools.partial(_proj_ln_kernel, alpha=alpha),
        grid=(bsz, t_len // tm),
        in_specs=[_row_spec(tm, a.shape[2]), _row_spec(tm, d), _const_spec(w.shape),
                  _const_spec(g.shape), _const_spec(b.shape)],
        out_specs=_row_spec(tm, d),
        out_shape=jax.ShapeDtypeStruct((bsz, t_len, d), F32),
        compiler_params=_params(),
        name="proj_ln",
    )(a, h, w, g, b)


def _sconv_kernel(h_ref, win_ref, cw_ref, wout_ref, g_ref, b_ref, halo_ref, o_ref, tail_ref,
                  st_ref, carry_ref, *, alpha, tm):
    t = pl.program_id(1)

    @pl.when(t == 0)
    def _():
        carry_ref[...] = halo_ref[...]

    hv = h_ref[0]
    hb = hv.astype(BF16)
    width = cw_ref.shape[1]
    b_gate = _dot(hb, win_ref[:, 0:width])
    cx = _dot(hb, win_ref[:, width:2 * width]) * _dot(hb, win_ref[:, 2 * width:3 * width])
    u = _causal_taps(st_ref, carry_ref[...], cx, cw_ref[...], tm)
    carry_ref[...] = cx[tm - HALO:tm, :]
    y = _dot((b_gate * u).astype(BF16), wout_ref[...])
    o_ref[0] = _layer_norm(alpha * hv + y, g_ref[...], b_ref[...])

    @pl.when(t == pl.num_programs(1) - 1)
    def _():
        tail_ref[0] = carry_ref[...]


def _sconv(h, w_in, conv_w, w_out, g, b, halo, alpha, tm):
    bsz, t_len, d = h.shape
    width = conv_w.shape[1]
    return pl.pallas_call(
        functools.partial(_sconv_kernel, alpha=alpha, tm=tm),
        grid=(bsz, t_len // tm),
        in_specs=[_row_spec(tm, d), _const_spec(w_in.shape), _const_spec(conv_w.shape),
                  _const_spec(w_out.shape), _const_spec(g.shape), _const_spec(b.shape),
                  _const_spec(halo.shape)],
        out_specs=(_row_spec(tm, d), _tail_spec(width)),
        out_shape=(jax.ShapeDtypeStruct((bsz, t_len, d), F32),
                   jax.ShapeDtypeStruct((bsz, HALO, width), F32)),
        scratch_shapes=[pltpu.VMEM((tm + HALO, width), F32), pltpu.VMEM((HALO, width), F32)],
        compiler_params=_params(),
        name="sconv_mixer",
    )(h, w_in, conv_w, w_out, g, b, halo)


def _ffn_kernel(h_ref, wu_ref, wg_ref, cw_ref, wd_ref, g_ref, b_ref, halo_ref, o_ref, tail_ref,
                st_ref, carry_ref, acc_ref, *, alpha, tm):
    t = pl.program_id(1)

    @pl.when(t == 0)
    def _():
        carry_ref[...] = halo_ref[...]

    hv = h_ref[0]
    hb = hv.astype(BF16)
    d_ff = cw_ref.shape[1]
    for c in range(d_ff // FFN_COL_CHUNK):
        cs = slice(c * FFN_COL_CHUNK, (c + 1) * FFN_COL_CHUNK)
        u = _dot(hb, wu_ref[:, cs])
        gate = _dot(hb, wg_ref[:, cs])
        uc = _causal_taps(st_ref, carry_ref[:, cs], u, cw_ref[:, cs], tm)
        carry_ref[:, cs] = u[tm - HALO:tm, :]
        act = (uc * _sigmoid(uc) * gate).astype(BF16)
        part = _dot(act, wd_ref[cs, :])
        if c == 0:
            acc_ref[...] = part
        else:
            acc_ref[...] += part
    o_ref[0] = _layer_norm(alpha * hv + acc_ref[...], g_ref[...], b_ref[...])

    @pl.when(t == pl.num_programs(1) - 1)
    def _():
        tail_ref[0] = carry_ref[...]


def _ffn(h, w_u, w_g, conv_w, w_down, g, b, halo, alpha, tm):
    bsz, t_len, d = h.shape
    d_ff = conv_w.shape[1]
    return pl.pallas_call(
        functools.partial(_ffn_kernel, alpha=alpha, tm=tm),
        grid=(bsz, t_len // tm),
        in_specs=[_row_spec(tm, d), _const_spec(w_u.shape), _const_spec(w_g.shape),
                  _const_spec(conv_w.shape), _const_spec(w_down.shape), _const_spec(g.shape),
                  _const_spec(b.shape), _const_spec(halo.shape)],
        out_specs=(_row_spec(tm, d), _tail_spec(d_ff)),
        out_shape=(jax.ShapeDtypeStruct((bsz, t_len, d), F32),
                   jax.ShapeDtypeStruct((bsz, HALO, d_ff), F32)),
        scratch_shapes=[pltpu.VMEM((tm + HALO, FFN_COL_CHUNK), F32), pltpu.VMEM((HALO, d_ff), F32),
                        pltpu.VMEM((tm, d), F32)],
        compiler_params=_params(),
        name="conv_ffn",
    )(h, w_u, w_g, conv_w, w_down, g, b, halo)


def _gdn_layer(hm, hx, w_in, conv_w, a_log, dt_bias, norm_w, w_out, ln_g, ln_b, alpha):
    w3 = 3 * GDN_WIDTH
    wqkv = w_in[:, :w3].astype(BF16)
    wz = w_in[:, w3:w3 + GDN_WIDTH].astype(BF16)
    wba = jnp.pad(w_in[:, w3 + GDN_WIDTH:], ((0, 0), (0, LANES - 2 * GDN_HEADS))).astype(BF16)
    lane_pad = (GDN_HEADS, LANES - 2 * GDN_HEADS)
    gscale = jnp.pad(-jnp.exp(a_log.astype(F32)), lane_pad)[None]
    dtb = jnp.pad(dt_bias.astype(F32), lane_pad)[None]
    nw = norm_w.astype(F32)[None]
    w_out_b = w_out.astype(BF16)
    g2, b2 = ln_g[None], ln_b[None]
    state_shape = (GDN_HEADS, GDN_HEAD_DIM, GDN_HEAD_DIM)

    qm, km, vm, zm, scm, tail_m = _gdn_in(hm, wqkv, wz, wba, conv_w, gscale, dtb,
                                          jnp.zeros((HALO, w3), F32), N_META)
    front = ((0, 0), (CHUNK - N_META, 0), (0, 0))
    om, state_m = _gdn(jnp.pad(qm, front), jnp.pad(km, front), jnp.pad(vm, front),
                       jnp.pad(zm, front), jnp.pad(scm, front), nw,
                       jnp.zeros(state_shape, F32), CHUNK)
    hm_new = _proj_ln(om[:, CHUNK - N_META:], hm, w_out_b, g2, b2, alpha, N_META)

    qx, kx, vx, zx, scx, _ = _gdn_in(hx, wqkv, wz, wba, conv_w, gscale, dtb, tail_m[0], ROW_TILE)
    ox, _ = _gdn(qx, kx, vx, zx, scx, nw, state_m[0], GDN_ROW_TILE)
    hx_new = _proj_ln(ox, hx, w_out_b, g2, b2, alpha, ROW_TILE)
    return hm_new, hx_new


def _sconv_layer(hm, hx, w_in, conv_w, w_out, ln_g, ln_b, alpha):
    w_in_b = w_in.astype(BF16)
    w_out_b = w_out.astype(BF16)
    g2, b2 = ln_g[None], ln_b[None]
    width = conv_w.shape[1]
    hm_new, tail_m = _sconv(hm, w_in_b, conv_w, w_out_b, g2, b2, jnp.zeros((HALO, width), F32),
                            alpha, N_META)
    hx_new, _ = _sconv(hx, w_in_b, conv_w, w_out_b, g2, b2, tail_m[0], alpha, ROW_TILE)
    return hm_new, hx_new


def _ffn_layer(hm, hx, w_up, conv_w, w_down, ln_g, ln_b, alpha):
    d_ff = conv_w.shape[1]
    w_u = w_up[:, :d_ff].astype(BF16)
    w_g = w_up[:, d_ff:].astype(BF16)
    w_d = w_down.astype(BF16)
    g2, b2 = ln_g[None], ln_b[None]
    hm_new, tail_m = _ffn(hm, w_u, w_g, conv_w, w_d, g2, b2, jnp.zeros((HALO, d_ff), F32),
                          alpha, N_META)
    hx_new, _ = _ffn(hx, w_u, w_g, conv_w, w_d, g2, b2, tail_m[0], alpha, ROW_TILE)
    return hm_new, hx_new


def kernel(x, meta, a_w_in, a_conv, a_log, a_dt_bias, a_norm, a_w_out, b_w_in, b_conv, b_w_out,
           ln_mix_g, ln_mix_b, ffn_w_up, ffn_conv, ffn_w_down, ln_ffn_g, ln_ffn_b):
    depth = ln_mix_g.shape[0]
    alpha = (2.0 * depth) ** 0.25
    assert x.shape[1] % ROW_TILE == 0 and meta.shape[0] == N_META
    hx = x
    hm = meta.astype(x.dtype)[None]
    for i in range(depth):
        j = i // 2
        if i % 2 == 0:
            hm, hx = _gdn_layer(hm, hx, a_w_in[j], a_conv[j], a_log[j], a_dt_bias[j], a_norm[j],
                                a_w_out[j], ln_mix_g[i], ln_mix_b[i], alpha)
        else:
            hm, hx = _sconv_layer(hm, hx, b_w_in[j], b_conv[j], b_w_out[j], ln_mix_g[i],
                                  ln_mix_b[i], alpha)
        hm, hx = _ffn_layer(hm, hx, ffn_w_up[i], ffn_conv[i], ffn_w_down[i], ln_ffn_g[i],
                            ln_ffn_b[i], alpha)
    return hx
```

```python
import functools

import jax
import jax.numpy as jnp
from jax import lax
from jax.experimental import pallas as pl
from jax.experimental.pallas import tpu as pltpu

F32 = jnp.float32
BF16 = jnp.bfloat16

N_META = 16
GDN_HEADS = 8
GDN_HEAD_DIM = 128
GDN_WIDTH = GDN_HEADS * GDN_HEAD_DIM
Q_SCALE = GDN_HEAD_DIM ** -0.5
CHUNK = 64
INV_BLOCK = 16
HALO = 8
LANES = 128
LN_EPS = 1e-5
RMS_EPS = 1e-6
L2_EPS = 1e-6
MASK_NEG = -1e30

ROW_TILE = 512
SUB_ROWS = 256
GDN_ROW_TILE = 256
QKV_COL_CHUNK = 512
EW_ROWS = 64
MXU_AHEAD = 2
QKV_SLOTS = MXU_AHEAD + 1
FFN_COL_CHUNK = 256
FFN_DOWN_GROUP = 4
SCONV_COL_CHUNK = 256
VMEM_LIMIT = 56 * 1024 * 1024


def _dot(a, b):
    return jnp.dot(a, b, preferred_element_type=F32)


def _dot_nt(a, b):
    return lax.dot_general(a, b, (((1,), (1,)), ((), ())), preferred_element_type=F32)


def _sigmoid(x):
    return 1.0 / (1.0 + jnp.exp(-x))


def _layer_norm(x, g, b):
    mu = jnp.mean(x, axis=-1, keepdims=True)
    xc = x - mu
    var = jnp.mean(xc * xc, axis=-1, keepdims=True)
    return xc * lax.rsqrt(var + LN_EPS) * g + b


def _const_spec(shape):
    nd = len(shape)
    return pl.BlockSpec(shape, lambda *_, _nd=nd: (0,) * _nd, pipeline_mode=pl.Buffered(1))


def _row_spec(tm, width):
    return pl.BlockSpec((1, tm, width), lambda b, t: (b, t, 0))


def _tail_spec(width):
    return pl.BlockSpec((1, HALO, width), lambda b, t: (b, 0, 0))


def _params():
    return pltpu.CompilerParams(dimension_semantics=("arbitrary", "arbitrary"),
                                vmem_limit_bytes=VMEM_LIMIT)


def _spread_taps(taps):
    return [jnp.broadcast_to(taps[j:j + 1, :], (HALO, taps.shape[1]))[None]
            for j in range(taps.shape[0])]


def _causal_taps(carry, cur, taps):
    width = len(taps)
    tm, c = cur.shape
    assert width - 1 <= HALO
    tiles = jnp.concatenate([carry, cur], axis=0).reshape(tm // HALO + 1, HALO, c)
    first = lax.broadcasted_iota(jnp.int32, (tm // HALO, HALO, c), 1) == 0
    acc = tiles * taps[0]
    for j in range(1, width):
        rot = pltpu.roll(acc, 1, 1)
        shifted = jnp.where(first, rot[:-1], rot[1:])
        if j < width - 1:
            acc = tiles * taps[j] + jnp.concatenate([rot[:1], shifted], axis=0)
        else:
            acc = tiles[1:] * taps[j] + shifted
    return acc.reshape(tm, c)


def _gdn_in_kernel(h_ref, wqkv_ref, wz_ref, wba_ref, cw_ref, gs_ref, dtb_ref, halo_ref,
                   q_ref, k_ref, v_ref, z_ref, sc_ref, tail_ref, st_ref, carry_ref, *, tm):
    t = pl.program_id(1)

    @pl.when(t == 0)
    def _():
        carry_ref[...] = halo_ref[...]

    hb = h_ref[0].astype(BF16)
    outs = (q_ref, k_ref, v_ref)
    heads_per_chunk = QKV_COL_CHUNK // GDN_HEAD_DIM
    chunks_per_out = GDN_WIDTH // QKV_COL_CHUNK
    n_chunks = 3 * chunks_per_out
    col_slice = lambda c: slice(c * QKV_COL_CHUNK, (c + 1) * QKV_COL_CHUNK)
    rb = min(EW_ROWS, tm)

    def project(c):
        st_ref[c % QKV_SLOTS, HALO:HALO + tm, :] = _dot(hb, wqkv_ref[:, col_slice(c)])

    def project_z(i):
        zs = slice(i * QKV_COL_CHUNK, (i + 1) * QKV_COL_CHUNK)
        z_ref[0, :, zs] = _dot(hb, wz_ref[:, zs]).astype(z_ref.dtype)

    for c in range(min(MXU_AHEAD, n_chunks)):
        project(c)
    ba = None
    for c in range(n_chunks):
        cs = col_slice(c)
        st = st_ref.at[c % QKV_SLOTS]
        if c + MXU_AHEAD < n_chunks:
            project(c + MXU_AHEAD)
        if c < chunks_per_out:
            project_z(c)
        elif ba is None:
            ba = _dot(hb, wba_ref[...])
        st[0:HALO, :] = carry_ref[:, cs]
        carry_ref[:, cs] = st[tm:tm + HALO, :]
        which = c // chunks_per_out
        for hh in range(heads_per_chunk):
            hs = slice(hh * GDN_HEAD_DIM, (hh + 1) * GDN_HEAD_DIM)
            col = (c % chunks_per_out) * QKV_COL_CHUNK + hh * GDN_HEAD_DIM
            taps = _spread_taps(cw_ref[:, c * QKV_COL_CHUNK + hh * GDN_HEAD_DIM:
                                       c * QKV_COL_CHUNK + (hh + 1) * GDN_HEAD_DIM])
            for r0 in range(0, tm, rb):
                blk = st[r0:r0 + rb + HALO, hs]
                y = _causal_taps(blk[:HALO], blk[HALO:], taps)
                y = y * _sigmoid(y)
                if which < 2:
                    inv = lax.rsqrt(jnp.sum(y * y, axis=-1, keepdims=True) + L2_EPS)
                    y = y * (inv * Q_SCALE if which == 0 else inv)
                outs[which][0, r0:r0 + rb, col:col + GDN_HEAD_DIM] = y.astype(outs[which].dtype)

    lane = lax.broadcasted_iota(jnp.int32, ba.shape, 1)
    xg = ba + dtb_ref[...]
    softplus = jnp.maximum(xg, 0.0) + jnp.log(1.0 + jnp.exp(-jnp.abs(xg)))
    sc_ref[0] = jnp.where(lane < GDN_HEADS, _sigmoid(ba), gs_ref[...] * softplus)

    @pl.when(t == pl.num_programs(1) - 1)
    def _():
        tail_ref[0] = carry_ref[...]


def _gdn_in(h, wqkv, wz, wba, conv_w, gscale, dtb, halo, tm):
    bsz, t_len, d = h.shape
    w3 = wqkv.shape[1]
    row = lambda width: _row_spec(tm, width)
    out_shape = (
        jax.ShapeDtypeStruct((bsz, t_len, GDN_WIDTH), F32),
        jax.ShapeDtypeStruct((bsz, t_len, GDN_WIDTH), F32),
        jax.ShapeDtypeStruct((bsz, t_len, GDN_WIDTH), F32),
        jax.ShapeDtypeStruct((bsz, t_len, GDN_WIDTH), F32),
        jax.ShapeDtypeStruct((bsz, t_len, LANES), F32),
        jax.ShapeDtypeStruct((bsz, HALO, w3), F32),
    )
    return pl.pallas_call(
        functools.partial(_gdn_in_kernel, tm=tm),
        grid=(bsz, t_len // tm),
        in_specs=[row(d), _const_spec(wqkv.shape), _const_spec(wz.shape), _const_spec(wba.shape),
                  _const_spec(conv_w.shape), _const_spec(gscale.shape), _const_spec(dtb.shape),
                  _const_spec(halo.shape)],
        out_specs=(row(GDN_WIDTH), row(GDN_WIDTH), row(GDN_WIDTH), row(GDN_WIDTH), row(LANES),
                   _tail_spec(w3)),
        out_shape=out_shape,
        scratch_shapes=[pltpu.VMEM((QKV_SLOTS, tm + HALO, QKV_COL_CHUNK), F32),
                        pltpu.VMEM((HALO, w3), F32)],
        compiler_params=_params(),
        name="gdn_in",
    )(h, wqkv, wz, wba, conv_w, gscale, dtb, halo)


def _mm_each(a_list, b_list):
    return [_dot(a.astype(BF16), b.astype(BF16)) for a, b in zip(a_list, b_list)]


def _unit_lower_inverse(ms, eye, blockdiag):
    d = [jnp.where(blockdiag, m, 0.0) for m in ms]
    lo = [jnp.where(blockdiag, 0.0, m) for m in ms]
    d2 = _mm_each(d, d)
    d4 = _mm_each(d2, d2)
    p = _mm_each([eye - x for x in d], [eye + x for x in d2])
    d8 = _mm_each(d4, d4)
    p = _mm_each(p, [eye + x for x in d4])
    p = _mm_each(p, [eye + x for x in d8])
    n = _mm_each(p, lo)
    n2 = _mm_each(n, n)
    qn = _mm_each([eye - x for x in n], [eye + x for x in n2])
    return _mm_each(qn, p)


def _gdn_kernel(q_ref, k_ref, v_ref, z_ref, sc_ref, nw_ref, s0_ref, o_ref, sfin_ref, state_ref,
                *, n_chunks):
    t = pl.program_id(1)

    @pl.when(t == 0)
    def _():
        state_ref[...] = s0_ref[...]

    ri = lax.broadcasted_iota(jnp.int32, (CHUNK, CHUNK), 0)
    ci = lax.broadcasted_iota(jnp.int32, (CHUNK, CHUNK), 1)
    causal = ri >= ci
    strict = ri > ci
    blockdiag = (ri // INV_BLOCK) == (ci // INV_BLOCK)
    eye = jnp.where(ri == ci, 1.0, 0.0).astype(F32)
    tril_b = jnp.where(causal, 1.0, 0.0).astype(BF16)
    nw = nw_ref[...]

    heads = range(GDN_HEADS)
    chunks = range(n_chunks)
    cols = [slice(h * GDN_HEAD_DIM, (h + 1) * GDN_HEAD_DIM) for h in heads]
    rows = [slice(c * CHUNK, (c + 1) * CHUNK) for c in chunks]
    items = [(c, h) for c in chunks for h in heads]

    scc = [sc_ref[0, r, :] for r in rows]
    hi = [x.astype(BF16) for x in scc]
    r1 = [x - y.astype(F32) for x, y in zip(scc, hi)]
    mid = [x.astype(BF16) for x in r1]
    low = [(x - y.astype(F32)).astype(BF16) for x, y in zip(r1, mid)]
    gc = [_dot(tril_b, a) + _dot(tril_b, b) + _dot(tril_b, c_)
          for a, b, c_ in zip(hi, mid, low)]
    gct = [x.T for x in gc]

    q = [q_ref[0, rows[c], cols[h]].astype(F32) for c, h in items]
    k = [k_ref[0, rows[c], cols[h]].astype(F32) for c, h in items]
    v = [v_ref[0, rows[c], cols[h]].astype(F32) for c, h in items]
    beta = [scc[c][:, h:h + 1] for c, h in items]
    gcol = [gc[c][:, GDN_HEADS + h:GDN_HEADS + h + 1] for c, h in items]
    grow = [gct[c][GDN_HEADS + h:GDN_HEADS + h + 1, :] for c, h in items]
    glast = [g[CHUNK - 1:CHUNK, :] for g in gcol]
    eg = [jnp.exp(g) for g in gcol]
    decay = [jnp.exp(jnp.where(causal, gc_ - gr_, MASK_NEG)) for gc_, gr_ in zip(gcol, grow)]
    kb = [k_ * b_ for k_, b_ in zip(k, beta)]
    kk_qk = [_dot_nt(jnp.concatenate([kb_, q_], axis=0).astype(BF16), k_.astype(BF16))
             for kb_, q_, k_ in zip(kb, q, k)]
    m = [jnp.where(strict, x[:CHUNK] * d_, 0.0) for x, d_ in zip(kk_qk, decay)]
    qk = [x[CHUNK:] * d_ for x, d_ in zip(kk_qk, decay)]
    tinv = _unit_lower_inverse(m, eye, blockdiag)
    rhs = [jnp.concatenate([v_ * b_, kb_ * e_], axis=1)
           for v_, b_, kb_, e_ in zip(v, beta, kb, eg)]
    uw = _mm_each(tinv, rhs)
    u = [x[:, :GDN_HEAD_DIM] for x in uw]
    lhs1 = [jnp.concatenate([x[:, GDN_HEAD_DIM:], q_ * e_], axis=0).astype(BF16)
            for x, q_, e_ in zip(uw, q, eg)]
    lhs2 = [jnp.concatenate([a, (k_ * jnp.exp(gl_ - gc_)).T], axis=0).astype(BF16)
            for a, k_, gl_, gc_ in zip(qk, k, glast, gcol)]
    sdec = [jnp.exp(g) for g in glast]

    for c in chunks:
        idx = [c * GDN_HEADS + h for h in heads]
        s = [state_ref[h] for h in heads]
        ws_qs = _mm_each([lhs1[i] for i in idx], s)
        v_new = [u[i] - y[:CHUNK] for i, y in zip(idx, ws_qs)]
        r2 = _mm_each([lhs2[i] for i in idx], v_new)
        for h in heads:
            state_ref[h] = s[h] * sdec[idx[h]] + r2[h][CHUNK:]
        for h in heads:
            o = ws_qs[h][CHUNK:] + r2[h][:CHUNK]
            o = o * lax.rsqrt(jnp.mean(o * o, axis=-1, keepdims=True) + RMS_EPS) * nw
            zh = z_ref[0, rows[c], cols[h]].astype(F32)
            o_ref[0, rows[c], cols[h]] = (o * (zh * _sigmoid(zh))).astype(o_ref.dtype)

    @pl.when(t == pl.num_programs(1) - 1)
    def _():
        sfin_ref[0] = state_ref[...]


def _gdn(q, k, v, z, sc, norm_w, state0, tm):
    bsz, t_len, _ = q.shape
    row = lambda width: _row_spec(tm, width)
    state_shape = (GDN_HEADS, GDN_HEAD_DIM, GDN_HEAD_DIM)
    return pl.pallas_call(
        functools.partial(_gdn_kernel, n_chunks=tm // CHUNK),
        grid=(bsz, t_len // tm),
        in_specs=[row(GDN_WIDTH), row(GDN_WIDTH), row(GDN_WIDTH), row(GDN_WIDTH), row(LANES),
                  _const_spec(norm_w.shape), _const_spec(state0.shape)],
        out_specs=(row(GDN_WIDTH),
                   pl.BlockSpec((1,) + state_shape, lambda b, t: (b, 0, 0, 0))),
        out_shape=(jax.ShapeDtypeStruct((bsz, t_len, GDN_WIDTH), BF16),
                   jax.ShapeDtypeStruct((bsz,) + state_shape, F32)),
        scratch_shapes=[pltpu.VMEM(state_shape, F32)],
        compiler_params=_params(),
        name="gdn_scan",
    )(q, k, v, z, sc, norm_w, state0)


def _proj_ln_kernel(a_ref, h_ref, w_ref, g_ref, b_ref, o_ref, *, alpha):
    tm = h_ref.shape[1]
    rs = min(SUB_ROWS, tm)
    for r0 in range(0, tm, rs):
        rows = slice(r0, r0 + rs)
        y = _dot(a_ref[0, rows, :], w_ref[...])
        o_ref[0, rows, :] = _layer_norm(alpha * h_ref[0, rows, :] + y, g_ref[...], b_ref[...])


def _proj_ln(a, h, w, g, b, alpha, tm):
    bsz, t_len, d = h.shape
    return pl.pallas_call(
        functools.partial(_proj_ln_kernel, alpha=alpha),
        grid=(bsz, t_len // tm),
        in_specs=[_row_spec(tm, a.shape[2]), _row_spec(tm, d), _const_spec(w.shape),
                  _const_spec(g.shape), _const_spec(b.shape)],
        out_specs=_row_spec(tm, d),
        out_shape=jax.ShapeDtypeStruct((bsz, t_len, d), F32),
        compiler_params=_params(),
        name="proj_ln",
    )(a, h, w, g, b)


def _sconv_kernel(h_ref, win_ref, cw_ref, wout_ref, g_ref, b_ref, halo_ref, o_ref, tail_ref,
                  carry_ref, act_ref, *, alpha, tm):
    t = pl.program_id(1)

    @pl.when(t == 0)
    def _():
        carry_ref[...] = halo_ref[...]

    width = cw_ref.shape[1]
    n_chunks = width // SCONV_COL_CHUNK
    rs = min(SUB_ROWS, tm)

    for r0 in range(0, tm, rs):
        rows = slice(r0, r0 + rs)
        hv = h_ref[0, rows, :]
        hb = hv.astype(BF16)

        def proj(c):
            lo = c * SCONV_COL_CHUNK
            return [_dot(hb, win_ref[:, part * width + lo:part * width + lo + SCONV_COL_CHUNK])
                    for part in range(3)]

        nxt = proj(0)
        for c in range(n_chunks):
            cs = slice(c * SCONV_COL_CHUNK, (c + 1) * SCONV_COL_CHUNK)
            b_gate, c_gate, xv = nxt
            if c + 1 < n_chunks:
                nxt = proj(c + 1)
            cx = c_gate * xv
            u = _causal_taps(carry_ref[:, cs], cx, _spread_taps(cw_ref[:, cs]))
            carry_ref[:, cs] = cx[rs - HALO:rs, :]
            act_ref[rows, cs] = (b_gate * u).astype(BF16)
        y = _dot(act_ref[rows, :], wout_ref[...])
        o_ref[0, rows, :] = _layer_norm(alpha * hv + y, g_ref[...], b_ref[...])

    @pl.when(t == pl.num_programs(1) - 1)
    def _():
        tail_ref[0] = carry_ref[...]


def _sconv(h, w_in, conv_w, w_out, g, b, halo, alpha, tm):
    bsz, t_len, d = h.shape
    width = conv_w.shape[1]
    return pl.pallas_call(
        functools.partial(_sconv_kernel, alpha=alpha, tm=tm),
        grid=(bsz, t_len // tm),
        in_specs=[_row_spec(tm, d), _const_spec(w_in.shape), _const_spec(conv_w.shape),
                  _const_spec(w_out.shape), _const_spec(g.shape), _const_spec(b.shape),
                  _const_spec(halo.shape)],
        out_specs=(_row_spec(tm, d), _tail_spec(width)),
        out_shape=(jax.ShapeDtypeStruct((bsz, t_len, d), F32),
                   jax.ShapeDtypeStruct((bsz, HALO, width), F32)),
        scratch_shapes=[pltpu.VMEM((HALO, width), F32), pltpu.VMEM((tm, width), BF16)],
        compiler_params=_params(),
        name="sconv_mixer",
    )(h, w_in, conv_w, w_out, g, b, halo)


def _ffn_kernel(h_ref, wu_ref, wg_ref, cw_ref, wd_ref, g_ref, b_ref, halo_ref, o_ref, tail_ref,
                carry_ref, act_ref, *, alpha, tm):
    t = pl.program_id(1)

    @pl.when(t == 0)
    def _():
        carry_ref[...] = halo_ref[...]

    d_ff = cw_ref.shape[1]
    n_chunks = d_ff // FFN_COL_CHUNK
    col_slice = lambda c: slice(c * FFN_COL_CHUNK, (c + 1) * FFN_COL_CHUNK)
    rs = min(SUB_ROWS, tm)

    for r0 in range(0, tm, rs):
        rows = slice(r0, r0 + rs)
        hv = h_ref[0, rows, :]
        hb = hv.astype(BF16)
        up = lambda c: (_dot(hb, wu_ref[:, col_slice(c)]), _dot(hb, wg_ref[:, col_slice(c)]))
        nxt = up(0)
        y = None
        group_start = 0
        for c in range(n_chunks):
            cs = col_slice(c)
            u, gate = nxt
            if c + 1 < n_chunks:
                nxt = up(c + 1)
            uc = _causal_taps(carry_ref[:, cs], u, _spread_taps(cw_ref[:, cs]))
            carry_ref[:, cs] = u[rs - HALO:rs, :]
            act_ref[rows, cs] = (uc * _sigmoid(uc) * gate).astype(BF16)
            if (c + 1 - group_start) == FFN_DOWN_GROUP or c + 1 == n_chunks:
                ks = slice(group_start * FFN_COL_CHUNK, (c + 1) * FFN_COL_CHUNK)
                part = _dot(act_ref[rows, ks], wd_ref[ks, :])
                y = part if y is None else y + part
                group_start = c + 1
        o_ref[0, rows, :] = _layer_norm(alpha * hv + y, g_ref[...], b_ref[...])

    @pl.when(t == pl.num_programs(1) - 1)
    def _():
        tail_ref[0] = carry_ref[...]


def _ffn(h, w_u, w_g, conv_w, w_down, g, b, halo, alpha, tm):
    bsz, t_len, d = h.shape
    d_ff = conv_w.shape[1]
    return pl.pallas_call(
        functools.partial(_ffn_kernel, alpha=alpha, tm=tm),
        grid=(bsz, t_len // tm),
        in_specs=[_row_spec(tm, d), _const_spec(w_u.shape), _const_spec(w_g.shape),
                  _const_spec(conv_w.shape), _const_spec(w_down.shape), _const_spec(g.shape),
                  _const_spec(b.shape), _const_spec(halo.shape)],
        out_specs=(_row_spec(tm, d), _tail_spec(d_ff)),
        out_shape=(jax.ShapeDtypeStruct((bsz, t_len, d), F32),
                   jax.ShapeDtypeStruct((bsz, HALO, d_ff), F32)),
        scratch_shapes=[pltpu.VMEM((HALO, d_ff), F32), pltpu.VMEM((tm, d_ff), BF16)],
        compiler_params=_params(),
        name="conv_ffn",
    )(h, w_u, w_g, conv_w, w_down, g, b, halo)


def _gdn_layer(hm, hx, w_in, conv_w, a_log, dt_bias, norm_w, w_out, ln_g, ln_b, alpha):
    w3 = 3 * GDN_WIDTH
    wqkv = w_in[:, :w3].astype(BF16)
    wz = w_in[:, w3:w3 + GDN_WIDTH].astype(BF16)
    wba = jnp.pad(w_in[:, w3 + GDN_WIDTH:], ((0, 0), (0, LANES - 2 * GDN_HEADS))).astype(BF16)
    lane_pad = (GDN_HEADS, LANES - 2 * GDN_HEADS)
    gscale = jnp.pad(-jnp.exp(a_log.astype(F32)), lane_pad)[None]
    dtb = jnp.pad(dt_bias.astype(F32), lane_pad)[None]
    nw = norm_w.astype(F32)[None]
    w_out_b = w_out.astype(BF16)
    g2, b2 = ln_g[None], ln_b[None]
    state_shape = (GDN_HEADS, GDN_HEAD_DIM, GDN_HEAD_DIM)

    qm, km, vm, zm, scm, tail_m = _gdn_in(hm, wqkv, wz, wba, conv_w, gscale, dtb,
                                          jnp.zeros((HALO, w3), F32), N_META)
    front = ((0, 0), (CHUNK - N_META, 0), (0, 0))
    om, state_m = _gdn(jnp.pad(qm, front), jnp.pad(km, front), jnp.pad(vm, front),
                       jnp.pad(zm, front), jnp.pad(scm, front), nw,
                       jnp.zeros(state_shape, F32), CHUNK)
    hm_new = _proj_ln(om[:, CHUNK - N_META:], hm, w_out_b, g2, b2, alpha, N_META)

    qx, kx, vx, zx, scx, _ = _gdn_in(hx, wqkv, wz, wba, conv_w, gscale, dtb, tail_m[0], ROW_TILE)
    ox, _ = _gdn(qx, kx, vx, zx, scx, nw, state_m[0], GDN_ROW_TILE)
    hx_new = _proj_ln(ox, hx, w_out_b, g2, b2, alpha, ROW_TILE)
    return hm_new, hx_new


def _sconv_layer(hm, hx, w_in, conv_w, w_out, ln_g, ln_b, alpha):
    w_in_b = w_in.astype(BF16)
    w_out_b = w_out.astype(BF16)
    g2, b2 = ln_g[None], ln_b[None]
    width = conv_w.shape[1]
    hm_new, tail_m = _sconv(hm, w_in_b, conv_w, w_out_b, g2, b2, jnp.zeros((HALO, width), F32),
                            alpha, N_META)
    hx_new, _ = _sconv(hx, w_in_b, conv_w, w_out_b, g2, b2, tail_m[0], alpha, ROW_TILE)
    return hm_new, hx_new


def _ffn_layer(hm, hx, w_up, conv_w, w_down, ln_g, ln_b, alpha):
    d_ff = conv_w.shape[1]
    w_u = w_up[:, :d_ff].astype(BF16)
    w_g = w_up[:, d_ff:].astype(BF16)
    w_d = w_down.astype(BF16)
    g2, b2 = ln_g[None], ln_b[None]
    hm_new, tail_m = _ffn(hm, w_u, w_g, conv_w, w_d, g2, b2, jnp.zeros((HALO, d_ff), F32),
                          alpha, N_META)
    hx_new, _ = _ffn(hx, w_u, w_g, conv_w, w_d, g2, b2, tail_m[0], alpha, ROW_TILE)
    return hm_new, hx_new


def kernel(x, meta, a_w_in, a_conv, a_log, a_dt_bias, a_norm, a_w_out, b_w_in, b_conv, b_w_out,
           ln_mix_g, ln_mix_b, ffn_w_up, ffn_conv, ffn_w_down, ln_ffn_g, ln_ffn_b):
    depth = ln_mix_g.shape[0]
    alpha = (2.0 * depth) ** 0.25
    assert x.shape[1] % ROW_TILE == 0 and meta.shape[0] == N_META
    hx = x
    hm = meta.astype(x.dtype)[None]
    for i in range(depth):
        j = i // 2
        if i % 2 == 0:
            hm, hx = _gdn_layer(hm, hx, a_w_in[j], a_conv[j], a_log[j], a_dt_bias[j], a_norm[j],
                                a_w_out[j], ln_mix_g[i], ln_mix_b[i], alpha)
        else:
            hm, hx = _sconv_layer(hm, hx, b_w_in[j], b_conv[j], b_w_out[j], ln_mix_g[i],
                                  ln_mix_b[i], alpha)
        hm, hx = _ffn_layer(hm, hx, ffn_w_up[i], ffn_conv[i], ffn_w_down[i], ln_ffn_g[i],
                            ln_ffn_b[i], alpha)
    return hx
```

```python
import functools

import jax
import jax.numpy as jnp
from jax import lax
from jax.experimental import pallas as pl
from jax.experimental.pallas import tpu as pltpu

F32 = jnp.float32
BF16 = jnp.bfloat16

N_META = 16
GDN_HEADS = 8
GDN_HEAD_DIM = 128
GDN_WIDTH = GDN_HEADS * GDN_HEAD_DIM
Q_SCALE = GDN_HEAD_DIM ** -0.5
CHUNK = 64
INV_BLOCK = 16
HALO = 8
LANES = 128
LN_EPS = 1e-5
RMS_EPS = 1e-6
L2_EPS = 1e-6
MASK_NEG = -1e30

ROW_TILE = 512
SUB_ROWS = 256
GDN_ROW_TILE = 512
GDN_GROUP = 4
QKV_COL_CHUNK = 512
EW_ROWS = 64
MXU_AHEAD = 2
QKV_SLOTS = MXU_AHEAD + 1
FFN_COL_CHUNK = 256
FFN_DOWN_GROUP = 4
SCONV_COL_CHUNK = 256
VMEM_LIMIT = 56 * 1024 * 1024


def _dot(a, b):
    return jnp.dot(a, b, preferred_element_type=F32)


def _dot_nt(a, b):
    return lax.dot_general(a, b, (((1,), (1,)), ((), ())), preferred_element_type=F32)


def _sigmoid(x):
    return 1.0 / (1.0 + jnp.exp(-x))


def _layer_norm(x, g, b):
    mu = jnp.mean(x, axis=-1, keepdims=True)
    xc = x - mu
    var = jnp.mean(xc * xc, axis=-1, keepdims=True)
    return xc * lax.rsqrt(var + LN_EPS) * g + b


def _const_spec(shape):
    nd = len(shape)
    return pl.BlockSpec(shape, lambda *_, _nd=nd: (0,) * _nd, pipeline_mode=pl.Buffered(1))


def _row_spec(tm, width):
    return pl.BlockSpec((1, tm, width), lambda b, t: (b, t, 0))


def _tail_spec(width):
    return pl.BlockSpec((1, HALO, width), lambda b, t: (b, 0, 0))


def _params():
    return pltpu.CompilerParams(dimension_semantics=("arbitrary", "arbitrary"),
                                vmem_limit_bytes=VMEM_LIMIT)


def _spread_taps(taps):
    return [jnp.broadcast_to(taps[j:j + 1, :], (HALO, taps.shape[1]))[None]
            for j in range(taps.shape[0])]


def _causal_taps(carry, cur, taps):
    width = len(taps)
    tm, c = cur.shape
    assert width - 1 <= HALO
    tiles = jnp.concatenate([carry, cur], axis=0).reshape(tm // HALO + 1, HALO, c)
    first = lax.broadcasted_iota(jnp.int32, (tm // HALO, HALO, c), 1) == 0
    acc = tiles * taps[0]
    for j in range(1, width):
        rot = pltpu.roll(acc, 1, 1)
        shifted = jnp.where(first, rot[:-1], rot[1:])
        if j < width - 1:
            acc = tiles * taps[j] + jnp.concatenate([rot[:1], shifted], axis=0)
        else:
            acc = tiles[1:] * taps[j] + shifted
    return acc.reshape(tm, c)


def _gdn_in_kernel(h_ref, w_ref, wba_ref, cw_ref, gs_ref, dtb_ref, halo_ref,
                   q_ref, k_ref, v_ref, z_ref, sc_ref, tail_ref, st_ref, carry_ref, *, tm):
    t = pl.program_id(1)

    @pl.when(t == 0)
    def _():
        carry_ref[...] = halo_ref[...]

    hb = h_ref[0].astype(BF16)
    outs = (q_ref, k_ref, v_ref)
    heads_per_chunk = QKV_COL_CHUNK // GDN_HEAD_DIM
    chunks_per_out = GDN_WIDTH // QKV_COL_CHUNK
    n_chunks = 3 * chunks_per_out
    col_slice = lambda c: slice(c * QKV_COL_CHUNK, (c + 1) * QKV_COL_CHUNK)
    rb = min(EW_ROWS, tm)

    def project(c):
        st_ref[c % QKV_SLOTS, HALO:HALO + tm, :] = _dot(hb, w_ref[:, col_slice(c)])

    def project_z(i):
        zs = slice(i * QKV_COL_CHUNK, (i + 1) * QKV_COL_CHUNK)
        ws = slice(3 * GDN_WIDTH + i * QKV_COL_CHUNK, 3 * GDN_WIDTH + (i + 1) * QKV_COL_CHUNK)
        z_ref[0, :, zs] = _dot(hb, w_ref[:, ws]).astype(z_ref.dtype)

    for c in range(min(MXU_AHEAD, n_chunks)):
        project(c)
    ba = None
    for c in range(n_chunks):
        cs = col_slice(c)
        st = st_ref.at[c % QKV_SLOTS]
        if c + MXU_AHEAD < n_chunks:
            project(c + MXU_AHEAD)
        if c < chunks_per_out:
            project_z(c)
        elif ba is None:
            ba = _dot(hb, wba_ref[...])
        st[0:HALO, :] = carry_ref[:, cs]
        carry_ref[:, cs] = st[tm:tm + HALO, :]
        which = c // chunks_per_out
        for hh in range(heads_per_chunk):
            hs = slice(hh * GDN_HEAD_DIM, (hh + 1) * GDN_HEAD_DIM)
            col = (c % chunks_per_out) * QKV_COL_CHUNK + hh * GDN_HEAD_DIM
            taps = _spread_taps(cw_ref[:, c * QKV_COL_CHUNK + hh * GDN_HEAD_DIM:
                                       c * QKV_COL_CHUNK + (hh + 1) * GDN_HEAD_DIM])
            for r0 in range(0, tm, rb):
                blk = st[r0:r0 + rb + HALO, hs]
                y = _causal_taps(blk[:HALO], blk[HALO:], taps)
                y = y * _sigmoid(y)
                if which < 2:
                    inv = lax.rsqrt(jnp.sum(y * y, axis=-1, keepdims=True) + L2_EPS)
                    y = y * (inv * Q_SCALE if which == 0 else inv)
                outs[which][0, r0:r0 + rb, col:col + GDN_HEAD_DIM] = y.astype(outs[which].dtype)

    lane = lax.broadcasted_iota(jnp.int32, ba.shape, 1)
    xg = ba + dtb_ref[...]
    softplus = jnp.maximum(xg, 0.0) + jnp.log(1.0 + jnp.exp(-jnp.abs(xg)))
    sc_ref[0] = jnp.where(lane < GDN_HEADS, _sigmoid(ba), gs_ref[...] * softplus)

    @pl.when(t == pl.num_programs(1) - 1)
    def _():
        tail_ref[0] = carry_ref[...]


def _gdn_in(h, w_qkvz, wba, conv_w, gscale, dtb, halo, tm):
    bsz, t_len, d = h.shape
    w3 = 3 * GDN_WIDTH
    row = lambda width: _row_spec(tm, width)
    out_shape = (
        jax.ShapeDtypeStruct((bsz, t_len, GDN_WIDTH), F32),
        jax.ShapeDtypeStruct((bsz, t_len, GDN_WIDTH), F32),
        jax.ShapeDtypeStruct((bsz, t_len, GDN_WIDTH), F32),
        jax.ShapeDtypeStruct((bsz, t_len, GDN_WIDTH), F32),
        jax.ShapeDtypeStruct((bsz, t_len, LANES), F32),
        jax.ShapeDtypeStruct((bsz, HALO, w3), F32),
    )
    return pl.pallas_call(
        functools.partial(_gdn_in_kernel, tm=tm),
        grid=(bsz, t_len // tm),
        in_specs=[row(d), _const_spec(w_qkvz.shape), _const_spec(wba.shape),
                  _const_spec(conv_w.shape), _const_spec(gscale.shape), _const_spec(dtb.shape),
                  _const_spec(halo.shape)],
        out_specs=(row(GDN_WIDTH), row(GDN_WIDTH), row(GDN_WIDTH), row(GDN_WIDTH), row(LANES),
                   _tail_spec(w3)),
        out_shape=out_shape,
        scratch_shapes=[pltpu.VMEM((QKV_SLOTS, tm + HALO, QKV_COL_CHUNK), F32),
                        pltpu.VMEM((HALO, w3), F32)],
        compiler_params=_params(),
        name="gdn_in",
    )(h, w_qkvz, wba, conv_w, gscale, dtb, halo)


def _block_diag(x):
    half = x.shape[1] // 2
    zero = jnp.zeros((x.shape[0], half), x.dtype)
    return jnp.concatenate([jnp.concatenate([x[:, :half], zero], axis=1),
                            jnp.concatenate([zero, x[:, half:]], axis=1)], axis=0)


def _mm_pairs(a_list, b_list):
    return [_dot(a.astype(BF16), _block_diag(b.astype(BF16))) for a, b in zip(a_list, b_list)]


def _unit_lower_inverse(ms, eye, blockdiag, out):
    d = [jnp.where(blockdiag, m, 0.0) for m in ms]
    lo = [jnp.where(blockdiag, 0.0, m) for m in ms]
    d2 = _mm_pairs(d, d)
    yield
    d4 = _mm_pairs(d2, d2)
    p = _mm_pairs([eye - x for x in d], [eye + x for x in d2])
    yield
    d8 = _mm_pairs(d4, d4)
    p = _mm_pairs(p, [eye + x for x in d4])
    yield
    p = _mm_pairs(p, [eye + x for x in d8])
    yield
    n = _mm_pairs(p, lo)
    yield
    n2 = _mm_pairs(n, n)
    yield
    qn = _mm_pairs([eye - x for x in n], [eye + x for x in n2])
    yield
    out[:] = _mm_pairs(qn, p)
    yield


def _interleave(major, minor, n_major, n_minor):
    done = 0
    for i, _ in enumerate(major):
        want = ((i + 1) * n_minor) // n_major
        while done < want and next(minor, StopIteration) is not StopIteration:
            done += 1
    for _ in minor:
        pass


PHASE1_STEPS = 11
PHASE2_STEPS_PER_CHUNK = 2


def _gdn_kernel(q_ref, k_ref, v_ref, z_ref, sc_ref, nw_ref, s0_ref, o_ref, sfin_ref, state_ref,
                *, n_chunks):
    t = pl.program_id(1)

    @pl.when(t == 0)
    def _():
        state_ref[...] = s0_ref[...]

    dh = GDN_HEAD_DIM
    ri = lax.broadcasted_iota(jnp.int32, (CHUNK, 2 * CHUNK), 0)
    li = lax.broadcasted_iota(jnp.int32, (CHUNK, 2 * CHUNK), 1)
    left = li < CHUNK
    ci = jnp.where(left, li, li - CHUNK)
    causal = ri >= ci
    strict = ri > ci
    blockdiag = (ri // INV_BLOCK) == (ci // INV_BLOCK)
    eye = jnp.where(ri == ci, 1.0, 0.0).astype(F32)
    ri1 = lax.broadcasted_iota(jnp.int32, (CHUNK, CHUNK), 0)
    ci1 = lax.broadcasted_iota(jnp.int32, (CHUNK, CHUNK), 1)
    tril_b = jnp.where(ri1 >= ci1, 1.0, 0.0).astype(BF16)
    nw = nw_ref[...]

    def per_head(col_a, col_b):
        return jnp.concatenate([jnp.broadcast_to(col_a, (CHUNK, dh)),
                                jnp.broadcast_to(col_b, (CHUNK, dh))], axis=1)

    pairs = range(GDN_HEADS // 2)
    rows = [slice(c * CHUNK, (c + 1) * CHUNK) for c in range(n_chunks)]
    pcols = [slice(2 * j * dh, (2 * j + 2) * dh) for j in pairs]

    def col(x, lane):
        return x[:, lane:lane + 1]

    def phase1(chunks, res):
        items = [(c, j) for c in chunks for j in pairs]
        scc, gc, gct2 = {}, {}, {}
        for c in chunks:
            scc[c] = sc_ref[0, rows[c], :]
            hi = scc[c].astype(BF16)
            r1 = scc[c] - hi.astype(F32)
            mid = r1.astype(BF16)
            low = (r1 - mid.astype(F32)).astype(BF16)
            gc[c] = _dot(tril_b, hi) + _dot(tril_b, mid) + _dot(tril_b, low)
            gct2[c] = jnp.concatenate([gc[c], gc[c]], axis=0).T
        yield
        q = [q_ref[0, rows[c], pcols[j]].astype(F32) for c, j in items]
        k = [k_ref[0, rows[c], pcols[j]].astype(F32) for c, j in items]
        v = [v_ref[0, rows[c], pcols[j]].astype(F32) for c, j in items]
        g_a = [col(gc[c], GDN_HEADS + 2 * j) for c, j in items]
        g_b = [col(gc[c], GDN_HEADS + 2 * j + 1) for c, j in items]
        gl_a = [x[CHUNK - 1:CHUNK, :] for x in g_a]
        gl_b = [x[CHUNK - 1:CHUNK, :] for x in g_b]
        beta = [per_head(col(scc[c], 2 * j), col(scc[c], 2 * j + 1)) for c, j in items]
        eg = [per_head(jnp.exp(a), jnp.exp(b)) for a, b in zip(g_a, g_b)]
        kd_scale = [per_head(jnp.exp(la - a), jnp.exp(lb - b))
                    for a, b, la, lb in zip(g_a, g_b, gl_a, gl_b)]
        gcol = [jnp.where(left, a, b) for a, b in zip(g_a, g_b)]
        grow = [jnp.where(left[:1], gct2[c][GDN_HEADS + 2 * j:GDN_HEADS + 2 * j + 1, :],
                          gct2[c][GDN_HEADS + 2 * j + 1:GDN_HEADS + 2 * j + 2, :])
                for c, j in items]
        decay = [jnp.exp(jnp.where(causal, gc_ - gr_, MASK_NEG)) for gc_, gr_ in zip(gcol, grow)]
        kb = [k_ * b_ for k_, b_ in zip(k, beta)]
        kk_qk = [_dot_nt(jnp.concatenate([kb_, q_], axis=0).astype(BF16),
                         _block_diag(k_.astype(BF16))) for kb_, q_, k_ in zip(kb, q, k)]
        yield
        m = [jnp.where(strict, x[:CHUNK] * d_, 0.0) for x, d_ in zip(kk_qk, decay)]
        qk = [x[CHUNK:] * d_ for x, d_ in zip(kk_qk, decay)]
        tinv = []
        yield from _unit_lower_inverse(m, eye, blockdiag, tinv)
        vb = [v_ * b_ for v_, b_ in zip(v, beta)]
        kbg = [kb_ * e_ for kb_, e_ in zip(kb, eg)]
        uw = [_dot(_block_diag(t_.astype(BF16)),
                   jnp.concatenate([jnp.concatenate([vb_[:, :dh], kbg_[:, :dh]], axis=1),
                                    jnp.concatenate([vb_[:, dh:], kbg_[:, dh:]], axis=1)],
                                   axis=0).astype(BF16))
              for t_, vb_, kbg_ in zip(tinv, vb, kbg)]
        for i, item in enumerate(items):
            kd = k[i] * kd_scale[i]
            kdt = jnp.concatenate([kd[:, :dh], kd[:, dh:]], axis=0).T
            res[item] = dict(
                u=(uw[i][:CHUNK, :dh], uw[i][CHUNK:, :dh]),
                lhs1=jnp.concatenate([jnp.concatenate([uw[i][:CHUNK, dh:], uw[i][CHUNK:, dh:]],
                                                      axis=1), q[i] * eg[i]], axis=0).astype(BF16),
                lhs2=jnp.concatenate([qk[i], kdt], axis=0).astype(BF16),
                sdec=(jnp.exp(gl_a[i]), jnp.exp(gl_b[i])))
        yield

    def phase2(chunks, res):
        for c in chunks:
            r = [res[c, j] for j in pairs]
            s = [(state_ref[2 * j], state_ref[2 * j + 1]) for j in pairs]
            ws_qs = [_dot(x["lhs1"], _block_diag(jnp.concatenate(s_, axis=1).astype(BF16)))
                     for x, s_ in zip(r, s)]
            yield
            v_new = [jnp.concatenate([x["u"][0] - y[:CHUNK, :dh], x["u"][1] - y[:CHUNK, dh:]],
                                     axis=1) for x, y in zip(r, ws_qs)]
            r2 = [_dot(x["lhs2"], _block_diag(vn.astype(BF16))) for x, vn in zip(r, v_new)]
            for j in pairs:
                state_ref[2 * j] = s[j][0] * r[j]["sdec"][0] + r2[j][CHUNK:, :dh]
                state_ref[2 * j + 1] = s[j][1] * r[j]["sdec"][1] + r2[j][CHUNK:, dh:]
            yield
            for j in pairs:
                o2 = ws_qs[j][CHUNK:] + r2[j][:CHUNK]
                for half in range(2):
                    o = o2[:, half * dh:(half + 1) * dh]
                    o = o * lax.rsqrt(jnp.mean(o * o, axis=-1, keepdims=True) + RMS_EPS) * nw
                    cs = slice((2 * j + half) * dh, (2 * j + half + 1) * dh)
                    zh = z_ref[0, rows[c], cs].astype(F32)
                    o_ref[0, rows[c], cs] = (o * (zh * _sigmoid(zh))).astype(o_ref.dtype)

    groups = [list(range(g, min(g + GDN_GROUP, n_chunks))) for g in range(0, n_chunks, GDN_GROUP)]
    res = {}
    for _ in phase1(groups[0], res):
        pass
    for prev, cur in zip(groups[:-1], groups[1:]):
        _interleave(phase1(cur, res), phase2(prev, res), PHASE1_STEPS,
                    PHASE2_STEPS_PER_CHUNK * len(prev))
    for _ in phase2(groups[-1], res):
        pass

    @pl.when(t == pl.num_programs(1) - 1)
    def _():
        sfin_ref[0] = state_ref[...]


def _gdn(q, k, v, z, sc, norm_w, state0, tm):
    bsz, t_len, _ = q.shape
    row = lambda width: _row_spec(tm, width)
    state_shape = (GDN_HEADS, GDN_HEAD_DIM, GDN_HEAD_DIM)
    return pl.pallas_call(
        functools.partial(_gdn_kernel, n_chunks=tm // CHUNK),
        grid=(bsz, t_len // tm),
        in_specs=[row(GDN_WIDTH), row(GDN_WIDTH), row(GDN_WIDTH), row(GDN_WIDTH), row(LANES),
                  _const_spec(norm_w.shape), _const_spec(state0.shape)],
        out_specs=(row(GDN_WIDTH),
                   pl.BlockSpec((1,) + state_shape, lambda b, t: (b, 0, 0, 0))),
        out_shape=(jax.ShapeDtypeStruct((bsz, t_len, GDN_WIDTH), BF16),
                   jax.ShapeDtypeStruct((bsz,) + state_shape, F32)),
        scratch_shapes=[pltpu.VMEM(state_shape, F32)],
        compiler_params=_params(),
        name="gdn_scan",
    )(q, k, v, z, sc, norm_w, state0)


def _sconv_kernel(h_ref, win_ref, cw_ref, wout_ref, g_ref, b_ref, halo_ref, o_ref, tail_ref,
                  carry_ref, act_ref, *, alpha, tm):
    t = pl.program_id(1)

    @pl.when(t == 0)
    def _():
        carry_ref[...] = halo_ref[...]

    width = cw_ref.shape[1]
    n_chunks = width // SCONV_COL_CHUNK
    rs = min(SUB_ROWS, tm)

    for r0 in range(0, tm, rs):
        rows = slice(r0, r0 + rs)
        hv = h_ref[0, rows, :]
        hb = hv.astype(BF16)

        def proj(c):
            lo = c * SCONV_COL_CHUNK
            return [_dot(hb, win_ref[:, part * width + lo:part * width + lo + SCONV_COL_CHUNK])
                    for part in range(3)]

        nxt = proj(0)
        for c in range(n_chunks):
            cs = slice(c * SCONV_COL_CHUNK, (c + 1) * SCONV_COL_CHUNK)
            b_gate, c_gate, xv = nxt
            if c + 1 < n_chunks:
                nxt = proj(c + 1)
            cx = c_gate * xv
            u = _causal_taps(carry_ref[:, cs], cx, _spread_taps(cw_ref[:, cs]))
            carry_ref[:, cs] = cx[rs - HALO:rs, :]
            act_ref[rows, cs] = (b_gate * u).astype(BF16)
        y = _dot(act_ref[rows, :], wout_ref[...])
        o_ref[0, rows, :] = _layer_norm(alpha * hv + y, g_ref[...], b_ref[...])

    @pl.when(t == pl.num_programs(1) - 1)
    def _():
        tail_ref[0] = carry_ref[...]


def _sconv(h, w_in, conv_w, w_out, g, b, halo, alpha, tm):
    bsz, t_len, d = h.shape
    width = conv_w.shape[1]
    return pl.pallas_call(
        functools.partial(_sconv_kernel, alpha=alpha, tm=tm),
        grid=(bsz, t_len // tm),
        in_specs=[_row_spec(tm, d), _const_spec(w_in.shape), _const_spec(conv_w.shape),
                  _const_spec(w_out.shape), _const_spec(g.shape), _const_spec(b.shape),
                  _const_spec(halo.shape)],
        out_specs=(_row_spec(tm, d), _tail_spec(width)),
        out_shape=(jax.ShapeDtypeStruct((bsz, t_len, d), F32),
                   jax.ShapeDtypeStruct((bsz, HALO, width), F32)),
        scratch_shapes=[pltpu.VMEM((HALO, width), F32), pltpu.VMEM((tm, width), BF16)],
        compiler_params=_params(),
        name="sconv_mixer",
    )(h, w_in, conv_w, w_out, g, b, halo)


def _ffn_kernel(*refs, alpha, tm, mixer_tail):
    if mixer_tail:
        a_ref, wo_ref, g1_ref, b1_ref = refs[:4]
        refs = refs[4:]
    (h_ref, wup_ref, cw_ref, wd_ref, g_ref, b_ref, halo_ref, o_ref, tail_ref,
     carry_ref, act_ref) = refs
    t = pl.program_id(1)

    @pl.when(t == 0)
    def _():
        carry_ref[...] = halo_ref[...]

    d_ff = cw_ref.shape[1]
    n_chunks = d_ff // FFN_COL_CHUNK
    col_slice = lambda c: slice(c * FFN_COL_CHUNK, (c + 1) * FFN_COL_CHUNK)
    rs = min(SUB_ROWS, tm)

    subs = [slice(r0, r0 + rs) for r0 in range(0, tm, rs)]
    if mixer_tail:
        proj = [_dot(a_ref[0, rows, :], wo_ref[...]) for rows in subs]
        for rows, y in zip(subs, proj):
            o_ref[0, rows, :] = _layer_norm(alpha * h_ref[0, rows, :] + y, g1_ref[...], b1_ref[...])
    x_ref = o_ref if mixer_tail else h_ref
    for rows in subs:
        hv = x_ref[0, rows, :]
        hb = hv.astype(BF16)
        up = lambda c: (_dot(hb, wup_ref[:, col_slice(c)]),
                        _dot(hb, wup_ref[:, d_ff + c * FFN_COL_CHUNK:d_ff + (c + 1) * FFN_COL_CHUNK]))
        nxt = up(0)
        y = None
        group_start = 0
        for c in range(n_chunks):
            cs = col_slice(c)
            u, gate = nxt
            if c + 1 < n_chunks:
                nxt = up(c + 1)
            uc = _causal_taps(carry_ref[:, cs], u, _spread_taps(cw_ref[:, cs]))
            carry_ref[:, cs] = u[rs - HALO:rs, :]
            act_ref[rows, cs] = (uc * _sigmoid(uc) * gate).astype(BF16)
            if (c + 1 - group_start) == FFN_DOWN_GROUP or c + 1 == n_chunks:
                ks = slice(group_start * FFN_COL_CHUNK, (c + 1) * FFN_COL_CHUNK)
                part = _dot(act_ref[rows, ks], wd_ref[ks, :])
                y = part if y is None else y + part
                group_start = c + 1
        o_ref[0, rows, :] = _layer_norm(alpha * hv + y, g_ref[...], b_ref[...])

    @pl.when(t == pl.num_programs(1) - 1)
    def _():
        tail_ref[0] = carry_ref[...]


def _ffn(h, w_up, conv_w, w_down, g, b, halo, alpha, tm, mixer=None):
    bsz, t_len, d = h.shape
    d_ff = conv_w.shape[1]
    operands = [h, w_up, conv_w, w_down, g, b, halo]
    in_specs = [_row_spec(tm, d)] + [_const_spec(x.shape) for x in operands[1:]]
    if mixer is not None:
        a, w_o, g1, b1 = mixer
        operands = [a, w_o, g1, b1] + operands
        in_specs = [_row_spec(tm, a.shape[2])] + [_const_spec(x.shape) for x in (w_o, g1, b1)] + in_specs
    return pl.pallas_call(
        functools.partial(_ffn_kernel, alpha=alpha, tm=tm, mixer_tail=mixer is not None),
        grid=(bsz, t_len // tm),
        in_specs=in_specs,
        out_specs=(_row_spec(tm, d), _tail_spec(d_ff)),
        out_shape=(jax.ShapeDtypeStruct((bsz, t_len, d), F32),
                   jax.ShapeDtypeStruct((bsz, HALO, d_ff), F32)),
        scratch_shapes=[pltpu.VMEM((HALO, d_ff), F32), pltpu.VMEM((tm, d_ff), BF16)],
        compiler_params=_params(),
        name="conv_ffn",
    )(*operands)


def _gdn_mixer(hm, hx, w_in, conv_w, a_log, dt_bias, norm_w):
    w3 = 3 * GDN_WIDTH
    w_qkvz = w_in.astype(BF16)
    wba = jnp.pad(w_in[:, w3 + GDN_WIDTH:], ((0, 0), (0, LANES - 2 * GDN_HEADS))).astype(BF16)
    lane_pad = (GDN_HEADS, LANES - 2 * GDN_HEADS)
    gscale = jnp.pad(-jnp.exp(a_log.astype(F32)), lane_pad)[None]
    dtb = jnp.pad(dt_bias.astype(F32), lane_pad)[None]
    nw = norm_w.astype(F32)[None]
    state_shape = (GDN_HEADS, GDN_HEAD_DIM, GDN_HEAD_DIM)

    qm, km, vm, zm, scm, tail_m = _gdn_in(hm, w_qkvz, wba, conv_w, gscale, dtb,
                                          jnp.zeros((HALO, w3), F32), N_META)
    front = ((0, 0), (CHUNK - N_META, 0), (0, 0))
    om, state_m = _gdn(jnp.pad(qm, front), jnp.pad(km, front), jnp.pad(vm, front),
                       jnp.pad(zm, front), jnp.pad(scm, front), nw,
                       jnp.zeros(state_shape, F32), CHUNK)

    qx, kx, vx, zx, scx, _ = _gdn_in(hx, w_qkvz, wba, conv_w, gscale, dtb, tail_m[0], ROW_TILE)
    ox, _ = _gdn(qx, kx, vx, zx, scx, nw, state_m[0], GDN_ROW_TILE)
    return om[:, CHUNK - N_META:], ox


def _sconv_layer(hm, hx, w_in, conv_w, w_out, ln_g, ln_b, alpha):
    w_in_b = w_in.astype(BF16)
    w_out_b = w_out.astype(BF16)
    g2, b2 = ln_g[None], ln_b[None]
    width = conv_w.shape[1]
    hm_new, tail_m = _sconv(hm, w_in_b, conv_w, w_out_b, g2, b2, jnp.zeros((HALO, width), F32),
                            alpha, N_META)
    hx_new, _ = _sconv(hx, w_in_b, conv_w, w_out_b, g2, b2, tail_m[0], alpha, ROW_TILE)
    return hm_new, hx_new


def _ffn_layer(hm, hx, w_up, conv_w, w_down, ln_g, ln_b, alpha, mixer=None):
    d_ff = conv_w.shape[1]
    w_u = w_up.astype(BF16)
    w_d = w_down.astype(BF16)
    g2, b2 = ln_g[None], ln_b[None]
    mix_m = mix_x = None
    if mixer is not None:
        am, ax, w_o, g1, b1 = mixer
        tail = (w_o.astype(BF16), g1[None], b1[None])
        mix_m, mix_x = (am,) + tail, (ax,) + tail
    hm_new, tail_m = _ffn(hm, w_u, conv_w, w_d, g2, b2, jnp.zeros((HALO, d_ff), F32),
                          alpha, N_META, mix_m)
    hx_new, _ = _ffn(hx, w_u, conv_w, w_d, g2, b2, tail_m[0], alpha, ROW_TILE, mix_x)
    return hm_new, hx_new


def kernel(x, meta, a_w_in, a_conv, a_log, a_dt_bias, a_norm, a_w_out, b_w_in, b_conv, b_w_out,
           ln_mix_g, ln_mix_b, ffn_w_up, ffn_conv, ffn_w_down, ln_ffn_g, ln_ffn_b):
    depth = ln_mix_g.shape[0]
    alpha = (2.0 * depth) ** 0.25
    assert x.shape[1] % ROW_TILE == 0 and meta.shape[0] == N_META
    hx = x
    hm = meta.astype(x.dtype)[None]
    for i in range(depth):
        j = i // 2
        mixer = None
        if i % 2 == 0:
            am, ax = _gdn_mixer(hm, hx, a_w_in[j], a_conv[j], a_log[j], a_dt_bias[j], a_norm[j])
            mixer = (am, ax, a_w_out[j], ln_mix_g[i], ln_mix_b[i])
        else:
            hm, hx = _sconv_layer(hm, hx, b_w_in[j], b_conv[j], b_w_out[j], ln_mix_g[i],
                                  ln_mix_b[i], alpha)
        hm, hx = _ffn_layer(hm, hx, ffn_w_up[i], ffn_conv[i], ffn_w_down[i], ln_ffn_g[i],
                            ln_ffn_b[i], alpha, mixer)
    return hx
```

```python
import functools

import jax
import jax.numpy as jnp
from jax import lax
from jax.experimental import pallas as pl
from jax.experimental.pallas import tpu as pltpu

F32 = jnp.float32
BF16 = jnp.bfloat16

N_META = 16
GDN_HEADS = 8
GDN_HEAD_DIM = 128
GDN_WIDTH = GDN_HEADS * GDN_HEAD_DIM
Q_SCALE = GDN_HEAD_DIM ** -0.5
CHUNK = 64
INV_BLOCK = 16
HALO = 8
LANES = 128
LN_EPS = 1e-5
RMS_EPS = 1e-6
L2_EPS = 1e-6
MASK_NEG = -1e30

ROW_TILE = 512
SUB_ROWS = 256
GDN_ROW_TILE = 512
GDN_GROUP = 4
QKV_COL_CHUNK = 512
EW_ROWS = 64
MXU_AHEAD = 2
QKV_SLOTS = MXU_AHEAD + 1
FFN_COL_CHUNK = 256
FFN_DOWN_GROUP = 4
SCONV_COL_CHUNK = 256
VMEM_LIMIT = 56 * 1024 * 1024


def _dot(a, b):
    return jnp.dot(a, b, preferred_element_type=F32)


def _dot_nt(a, b):
    return lax.dot_general(a, b, (((1,), (1,)), ((), ())), preferred_element_type=F32)


def _sigmoid(x):
    return 1.0 / (1.0 + jnp.exp(-x))


def _layer_norm(x, g, b):
    mu = jnp.mean(x, axis=-1, keepdims=True)
    xc = x - mu
    var = jnp.mean(xc * xc, axis=-1, keepdims=True)
    return xc * lax.rsqrt(var + LN_EPS) * g + b


def _const_spec(shape):
    nd = len(shape)
    return pl.BlockSpec(shape, lambda *_, _nd=nd: (0,) * _nd, pipeline_mode=pl.Buffered(1))


def _layer_spec(stacked, layer):
    nd = stacked.ndim - 1
    return pl.BlockSpec((None,) + stacked.shape[1:], lambda *_, _nd=nd: (layer,) + (0,) * _nd,
                        pipeline_mode=pl.Buffered(1))


def _row_spec(tm, width):
    return pl.BlockSpec((1, tm, width), lambda b, t: (b, t, 0))


def _tail_spec(width):
    return pl.BlockSpec((1, HALO, width), lambda b, t: (b, 0, 0))


def _params():
    return pltpu.CompilerParams(dimension_semantics=("arbitrary", "arbitrary"),
                                vmem_limit_bytes=VMEM_LIMIT)


def _spread_taps(taps):
    return [jnp.broadcast_to(taps[j:j + 1, :], (HALO, taps.shape[1]))[None]
            for j in range(taps.shape[0])]


def _causal_taps(carry, cur, taps):
    width = len(taps)
    tm, c = cur.shape
    assert width - 1 <= HALO
    tiles = jnp.concatenate([carry, cur], axis=0).reshape(tm // HALO + 1, HALO, c)
    first = lax.broadcasted_iota(jnp.int32, (tm // HALO, HALO, c), 1) == 0
    acc = tiles * taps[0]
    for j in range(1, width):
        rot = pltpu.roll(acc, 1, 1)
        shifted = jnp.where(first, rot[:-1], rot[1:])
        if j < width - 1:
            acc = tiles * taps[j] + jnp.concatenate([rot[:1], shifted], axis=0)
        else:
            acc = tiles[1:] * taps[j] + shifted
    return acc.reshape(tm, c)


def _gdn_in_kernel(h_ref, w_ref, wba_ref, cw_ref, gs_ref, dtb_ref, halo_ref,
                   q_ref, k_ref, v_ref, z_ref, sc_ref, tail_ref, st_ref, carry_ref, *, tm):
    t = pl.program_id(1)

    @pl.when(t == 0)
    def _():
        carry_ref[...] = halo_ref[...]

    hb = h_ref[0].astype(BF16)
    outs = (q_ref, k_ref, v_ref)
    heads_per_chunk = QKV_COL_CHUNK // GDN_HEAD_DIM
    chunks_per_out = GDN_WIDTH // QKV_COL_CHUNK
    n_chunks = 3 * chunks_per_out
    col_slice = lambda c: slice(c * QKV_COL_CHUNK, (c + 1) * QKV_COL_CHUNK)
    rb = min(EW_ROWS, tm)

    def project(c):
        st_ref[c % QKV_SLOTS, HALO:HALO + tm, :] = _dot(hb, w_ref[:, col_slice(c)])

    def project_z(i):
        zs = slice(i * QKV_COL_CHUNK, (i + 1) * QKV_COL_CHUNK)
        ws = slice(3 * GDN_WIDTH + i * QKV_COL_CHUNK, 3 * GDN_WIDTH + (i + 1) * QKV_COL_CHUNK)
        z_ref[0, :, zs] = _dot(hb, w_ref[:, ws]).astype(z_ref.dtype)

    for c in range(min(MXU_AHEAD, n_chunks)):
        project(c)
    ba = None
    for c in range(n_chunks):
        cs = col_slice(c)
        st = st_ref.at[c % QKV_SLOTS]
        if c + MXU_AHEAD < n_chunks:
            project(c + MXU_AHEAD)
        if c < chunks_per_out:
            project_z(c)
        elif ba is None:
            ba = _dot(hb, wba_ref[...])
        st[0:HALO, :] = carry_ref[:, cs]
        carry_ref[:, cs] = st[tm:tm + HALO, :]
        which = c // chunks_per_out
        for hh in range(heads_per_chunk):
            hs = slice(hh * GDN_HEAD_DIM, (hh + 1) * GDN_HEAD_DIM)
            col = (c % chunks_per_out) * QKV_COL_CHUNK + hh * GDN_HEAD_DIM
            taps = _spread_taps(cw_ref[:, c * QKV_COL_CHUNK + hh * GDN_HEAD_DIM:
                                       c * QKV_COL_CHUNK + (hh + 1) * GDN_HEAD_DIM])
            for r0 in range(0, tm, rb):
                blk = st[r0:r0 + rb + HALO, hs]
                y = _causal_taps(blk[:HALO], blk[HALO:], taps)
                y = y * _sigmoid(y)
                if which < 2:
                    inv = lax.rsqrt(jnp.sum(y * y, axis=-1, keepdims=True) + L2_EPS)
                    y = y * (inv * Q_SCALE if which == 0 else inv)
                outs[which][0, r0:r0 + rb, col:col + GDN_HEAD_DIM] = y.astype(outs[which].dtype)

    lane = lax.broadcasted_iota(jnp.int32, ba.shape, 1)
    xg = ba + dtb_ref[...]
    softplus = jnp.maximum(xg, 0.0) + jnp.log(1.0 + jnp.exp(-jnp.abs(xg)))
    sc_ref[0] = jnp.where(lane < GDN_HEADS, _sigmoid(ba), gs_ref[...] * softplus)

    @pl.when(t == pl.num_programs(1) - 1)
    def _():
        tail_ref[0] = carry_ref[...]


def _gdn_in(h, w_in, layer, wba, conv_w, gscale, dtb, halo, tm):
    bsz, t_len, d = h.shape
    w3 = 3 * GDN_WIDTH
    row = lambda width: _row_spec(tm, width)
    out_shape = (
        jax.ShapeDtypeStruct((bsz, t_len, GDN_WIDTH), BF16),
        jax.ShapeDtypeStruct((bsz, t_len, GDN_WIDTH), BF16),
        jax.ShapeDtypeStruct((bsz, t_len, GDN_WIDTH), BF16),
        jax.ShapeDtypeStruct((bsz, t_len, GDN_WIDTH), BF16),
        jax.ShapeDtypeStruct((bsz, t_len, LANES), F32),
        jax.ShapeDtypeStruct((bsz, HALO, w3), F32),
    )
    return pl.pallas_call(
        functools.partial(_gdn_in_kernel, tm=tm),
        grid=(bsz, t_len // tm),
        in_specs=[row(d), _layer_spec(w_in, layer), _const_spec(wba.shape),
                  _const_spec(conv_w.shape), _const_spec(gscale.shape), _const_spec(dtb.shape),
                  _const_spec(halo.shape)],
        out_specs=(row(GDN_WIDTH), row(GDN_WIDTH), row(GDN_WIDTH), row(GDN_WIDTH), row(LANES),
                   _tail_spec(w3)),
        out_shape=out_shape,
        scratch_shapes=[pltpu.VMEM((QKV_SLOTS, tm + HALO, QKV_COL_CHUNK), F32),
                        pltpu.VMEM((HALO, w3), F32)],
        compiler_params=_params(),
        name="gdn_in",
    )(h, w_in, wba, conv_w, gscale, dtb, halo)


def _block_diag(x):
    half = x.shape[1] // 2
    zero = jnp.zeros((x.shape[0], half), x.dtype)
    return jnp.concatenate([jnp.concatenate([x[:, :half], zero], axis=1),
                            jnp.concatenate([zero, x[:, half:]], axis=1)], axis=0)


def _mm_pairs(a_list, b_list):
    return [_dot(a.astype(BF16), _block_diag(b.astype(BF16))) for a, b in zip(a_list, b_list)]


def _unit_lower_inverse(ms, eye, blockdiag, out):
    d = [jnp.where(blockdiag, m, 0.0) for m in ms]
    lo = [jnp.where(blockdiag, 0.0, m) for m in ms]
    d2 = _mm_pairs(d, d)
    yield
    d4 = _mm_pairs(d2, d2)
    p = _mm_pairs([eye - x for x in d], [eye + x for x in d2])
    yield
    d8 = _mm_pairs(d4, d4)
    p = _mm_pairs(p, [eye + x for x in d4])
    yield
    p = _mm_pairs(p, [eye + x for x in d8])
    yield
    n = _mm_pairs(p, lo)
    yield
    n2 = _mm_pairs(n, n)
    yield
    qn = _mm_pairs([eye - x for x in n], [eye + x for x in n2])
    yield
    out[:] = _mm_pairs(qn, p)
    yield


def _interleave(major, minor, n_major, n_minor):
    done = 0
    for i, _ in enumerate(major):
        want = ((i + 1) * n_minor) // n_major
        while done < want and next(minor, StopIteration) is not StopIteration:
            done += 1
    for _ in minor:
        pass


PHASE1_STEPS = 11
PHASE2_STEPS_PER_CHUNK = 2


def _gdn_kernel(q_ref, k_ref, v_ref, z_ref, sc_ref, nw_ref, s0_ref, o_ref, sfin_ref, state_ref,
                *, n_chunks):
    t = pl.program_id(1)

    @pl.when(t == 0)
    def _():
        state_ref[...] = s0_ref[...]

    dh = GDN_HEAD_DIM
    ri = lax.broadcasted_iota(jnp.int32, (CHUNK, 2 * CHUNK), 0)
    li = lax.broadcasted_iota(jnp.int32, (CHUNK, 2 * CHUNK), 1)
    left = li < CHUNK
    ci = jnp.where(left, li, li - CHUNK)
    causal = ri >= ci
    strict = ri > ci
    blockdiag = (ri // INV_BLOCK) == (ci // INV_BLOCK)
    eye = jnp.where(ri == ci, 1.0, 0.0).astype(F32)
    ri1 = lax.broadcasted_iota(jnp.int32, (CHUNK, CHUNK), 0)
    ci1 = lax.broadcasted_iota(jnp.int32, (CHUNK, CHUNK), 1)
    tril_b = jnp.where(ri1 >= ci1, 1.0, 0.0).astype(BF16)
    nw = nw_ref[...]

    def per_head(col_a, col_b):
        return jnp.concatenate([jnp.broadcast_to(col_a, (CHUNK, dh)),
                                jnp.broadcast_to(col_b, (CHUNK, dh))], axis=1)

    pairs = range(GDN_HEADS // 2)
    rows = [slice(c * CHUNK, (c + 1) * CHUNK) for c in range(n_chunks)]
    pcols = [slice(2 * j * dh, (2 * j + 2) * dh) for j in pairs]

    def col(x, lane):
        return x[:, lane:lane + 1]

    def phase1(chunks, res):
        items = [(c, j) for c in chunks for j in pairs]
        scc, gc, gct2 = {}, {}, {}
        for c in chunks:
            scc[c] = sc_ref[0, rows[c], :]
            hi = scc[c].astype(BF16)
            r1 = scc[c] - hi.astype(F32)
            mid = r1.astype(BF16)
            low = (r1 - mid.astype(F32)).astype(BF16)
            gc[c] = _dot(tril_b, hi) + _dot(tril_b, mid) + _dot(tril_b, low)
            gct2[c] = jnp.concatenate([gc[c], gc[c]], axis=0).T
        yield
        q = [q_ref[0, rows[c], pcols[j]].astype(F32) for c, j in items]
        k_b = [k_ref[0, rows[c], pcols[j]] for c, j in items]
        k = [x.astype(F32) for x in k_b]
        v = [v_ref[0, rows[c], pcols[j]].astype(F32) for c, j in items]
        g_a = [col(gc[c], GDN_HEADS + 2 * j) for c, j in items]
        g_b = [col(gc[c], GDN_HEADS + 2 * j + 1) for c, j in items]
        gl_a = [x[CHUNK - 1:CHUNK, :] for x in g_a]
        gl_b = [x[CHUNK - 1:CHUNK, :] for x in g_b]
        beta = [per_head(col(scc[c], 2 * j), col(scc[c], 2 * j + 1)) for c, j in items]
        eg = [per_head(jnp.exp(a), jnp.exp(b)) for a, b in zip(g_a, g_b)]
        kd_scale = [per_head(jnp.exp(la - a), jnp.exp(lb - b))
                    for a, b, la, lb in zip(g_a, g_b, gl_a, gl_b)]
        gcol = [jnp.where(left, a, b) for a, b in zip(g_a, g_b)]
        grow = [jnp.where(left[:1], gct2[c][GDN_HEADS + 2 * j:GDN_HEADS + 2 * j + 1, :],
                          gct2[c][GDN_HEADS + 2 * j + 1:GDN_HEADS + 2 * j + 2, :])
                for c, j in items]
        decay = [jnp.exp(jnp.where(causal, gc_ - gr_, MASK_NEG)) for gc_, gr_ in zip(gcol, grow)]
        kb = [k_ * b_ for k_, b_ in zip(k, beta)]
        kk_qk = [_dot_nt(jnp.concatenate([kb_, q_], axis=0).astype(BF16), _block_diag(k_))
                 for kb_, q_, k_ in zip(kb, q, k_b)]
        yield
        m = [jnp.where(strict, x[:CHUNK] * d_, 0.0) for x, d_ in zip(kk_qk, decay)]
        qk = [x[CHUNK:] * d_ for x, d_ in zip(kk_qk, decay)]
        tinv = []
        yield from _unit_lower_inverse(m, eye, blockdiag, tinv)
        vb = [v_ * b_ for v_, b_ in zip(v, beta)]
        kbg = [kb_ * e_ for kb_, e_ in zip(kb, eg)]
        uw = [_dot(_block_diag(t_.astype(BF16)),
                   jnp.concatenate([jnp.concatenate([vb_[:, :dh], kbg_[:, :dh]], axis=1),
                                    jnp.concatenate([vb_[:, dh:], kbg_[:, dh:]], axis=1)],
                                   axis=0).astype(BF16))
              for t_, vb_, kbg_ in zip(tinv, vb, kbg)]
        for i, item in enumerate(items):
            kd = k[i] * kd_scale[i]
            kdt = jnp.concatenate([kd[:, :dh], kd[:, dh:]], axis=0).T
            res[item] = dict(
                u=(uw[i][:CHUNK, :dh], uw[i][CHUNK:, :dh]),
                lhs1=jnp.concatenate([jnp.concatenate([uw[i][:CHUNK, dh:], uw[i][CHUNK:, dh:]],
                                                      axis=1), q[i] * eg[i]], axis=0).astype(BF16),
                lhs2=jnp.concatenate([qk[i], kdt], axis=0).astype(BF16),
                sdec=(jnp.exp(gl_a[i]), jnp.exp(gl_b[i])))
        yield

    def phase2(chunks, res):
        for c in chunks:
            r = [res[c, j] for j in pairs]
            s = [(state_ref[2 * j], state_ref[2 * j + 1]) for j in pairs]
            ws_qs = [_dot(x["lhs1"], _block_diag(jnp.concatenate(s_, axis=1).astype(BF16)))
                     for x, s_ in zip(r, s)]
            yield
            v_new = [jnp.concatenate([x["u"][0] - y[:CHUNK, :dh], x["u"][1] - y[:CHUNK, dh:]],
                                     axis=1) for x, y in zip(r, ws_qs)]
            r2 = [_dot(x["lhs2"], _block_diag(vn.astype(BF16))) for x, vn in zip(r, v_new)]
            for j in pairs:
                state_ref[2 * j] = s[j][0] * r[j]["sdec"][0] + r2[j][CHUNK:, :dh]
                state_ref[2 * j + 1] = s[j][1] * r[j]["sdec"][1] + r2[j][CHUNK:, dh:]
            yield
            for j in pairs:
                o2 = ws_qs[j][CHUNK:] + r2[j][:CHUNK]
                for half in range(2):
                    o = o2[:, half * dh:(half + 1) * dh]
                    o = o * lax.rsqrt(jnp.mean(o * o, axis=-1, keepdims=True) + RMS_EPS) * nw
                    cs = slice((2 * j + half) * dh, (2 * j + half + 1) * dh)
                    zh = z_ref[0, rows[c], cs].astype(F32)
                    o_ref[0, rows[c], cs] = (o * (zh * _sigmoid(zh))).astype(o_ref.dtype)

    groups = [list(range(g, min(g + GDN_GROUP, n_chunks))) for g in range(0, n_chunks, GDN_GROUP)]
    res = {}
    for _ in phase1(groups[0], res):
        pass
    for prev, cur in zip(groups[:-1], groups[1:]):
        _interleave(phase1(cur, res), phase2(prev, res), PHASE1_STEPS,
                    PHASE2_STEPS_PER_CHUNK * len(prev))
    for _ in phase2(groups[-1], res):
        pass

    @pl.when(t == pl.num_programs(1) - 1)
    def _():
        sfin_ref[0] = state_ref[...]


def _gdn(q, k, v, z, sc, norm_w, state0, tm):
    bsz, t_len, _ = q.shape
    row = lambda width: _row_spec(tm, width)
    state_shape = (GDN_HEADS, GDN_HEAD_DIM, GDN_HEAD_DIM)
    return pl.pallas_call(
        functools.partial(_gdn_kernel, n_chunks=tm // CHUNK),
        grid=(bsz, t_len // tm),
        in_specs=[row(GDN_WIDTH), row(GDN_WIDTH), row(GDN_WIDTH), row(GDN_WIDTH), row(LANES),
                  _const_spec(norm_w.shape), _const_spec(state0.shape)],
        out_specs=(row(GDN_WIDTH),
                   pl.BlockSpec((1,) + state_shape, lambda b, t: (b, 0, 0, 0))),
        out_shape=(jax.ShapeDtypeStruct((bsz, t_len, GDN_WIDTH), BF16),
                   jax.ShapeDtypeStruct((bsz,) + state_shape, F32)),
        scratch_shapes=[pltpu.VMEM(state_shape, F32)],
        compiler_params=_params(),
        name="gdn_scan",
    )(q, k, v, z, sc, norm_w, state0)


def _sconv_kernel(h_ref, win_ref, cw_ref, wout_ref, g_ref, b_ref, halo_ref, o_ref, tail_ref,
                  carry_ref, act_ref, *, alpha, tm):
    t = pl.program_id(1)

    @pl.when(t == 0)
    def _():
        carry_ref[...] = halo_ref[...]

    width = cw_ref.shape[1]
    n_chunks = width // SCONV_COL_CHUNK
    rs = min(SUB_ROWS, tm)

    for r0 in range(0, tm, rs):
        rows = slice(r0, r0 + rs)
        hv = h_ref[0, rows, :]
        hb = hv.astype(BF16)

        def proj(c):
            lo = c * SCONV_COL_CHUNK
            return [_dot(hb, win_ref[:, part * width + lo:part * width + lo + SCONV_COL_CHUNK])
                    for part in range(3)]

        nxt = proj(0)
        for c in range(n_chunks):
            cs = slice(c * SCONV_COL_CHUNK, (c + 1) * SCONV_COL_CHUNK)
            b_gate, c_gate, xv = nxt
            if c + 1 < n_chunks:
                nxt = proj(c + 1)
            cx = c_gate * xv
            u = _causal_taps(carry_ref[:, cs], cx, _spread_taps(cw_ref[:, cs]))
            carry_ref[:, cs] = cx[rs - HALO:rs, :]
            act_ref[rows, cs] = (b_gate * u).astype(BF16)
        y = _dot(act_ref[rows, :], wout_ref[...])
        o_ref[0, rows, :] = _layer_norm(alpha * hv + y, g_ref[...], b_ref[...])

    @pl.when(t == pl.num_programs(1) - 1)
    def _():
        tail_ref[0] = carry_ref[...]


def _sconv(h, w_in, w_out, layer, conv_w, g, b, halo, alpha, tm):
    bsz, t_len, d = h.shape
    width = conv_w.shape[1]
    return pl.pallas_call(
        functools.partial(_sconv_kernel, alpha=alpha, tm=tm),
        grid=(bsz, t_len // tm),
        in_specs=[_row_spec(tm, d), _layer_spec(w_in, layer), _const_spec(conv_w.shape),
                  _layer_spec(w_out, layer), _const_spec(g.shape), _const_spec(b.shape),
                  _const_spec(halo.shape)],
        out_specs=(_row_spec(tm, d), _tail_spec(width)),
        out_shape=(jax.ShapeDtypeStruct((bsz, t_len, d), F32),
                   jax.ShapeDtypeStruct((bsz, HALO, width), F32)),
        scratch_shapes=[pltpu.VMEM((HALO, width), F32), pltpu.VMEM((tm, width), BF16)],
        compiler_params=_params(),
        name="sconv_mixer",
    )(h, w_in, conv_w, w_out, g, b, halo)


def _ffn_kernel(*refs, alpha, tm, mixer_tail):
    if mixer_tail:
        a_ref, wo_ref, g1_ref, b1_ref = refs[:4]
        refs = refs[4:]
    (h_ref, wup_ref, cw_ref, wd_ref, g_ref, b_ref, halo_ref, o_ref, tail_ref,
     carry_ref, act_ref) = refs
    t = pl.program_id(1)

    @pl.when(t == 0)
    def _():
        carry_ref[...] = halo_ref[...]

    d_ff = cw_ref.shape[1]
    n_chunks = d_ff // FFN_COL_CHUNK
    col_slice = lambda c: slice(c * FFN_COL_CHUNK, (c + 1) * FFN_COL_CHUNK)
    rs = min(SUB_ROWS, tm)

    subs = [slice(r0, r0 + rs) for r0 in range(0, tm, rs)]
    if mixer_tail:
        proj = [_dot(a_ref[0, rows, :], wo_ref[...]) for rows in subs]
        for rows, y in zip(subs, proj):
            o_ref[0, rows, :] = _layer_norm(alpha * h_ref[0, rows, :] + y, g1_ref[...], b1_ref[...])
    x_ref = o_ref if mixer_tail else h_ref
    for rows in subs:
        hv = x_ref[0, rows, :]
        hb = hv.astype(BF16)
        up = lambda c: (_dot(hb, wup_ref[:, col_slice(c)]),
                        _dot(hb, wup_ref[:, d_ff + c * FFN_COL_CHUNK:d_ff + (c + 1) * FFN_COL_CHUNK]))
        nxt = up(0)
        y = None
        group_start = 0
        for c in range(n_chunks):
            cs = col_slice(c)
            u, gate = nxt
            if c + 1 < n_chunks:
                nxt = up(c + 1)
            uc = _causal_taps(carry_ref[:, cs], u, _spread_taps(cw_ref[:, cs]))
            carry_ref[:, cs] = u[rs - HALO:rs, :]
            act_ref[rows, cs] = (uc * _sigmoid(uc) * gate).astype(BF16)
            if (c + 1 - group_start) == FFN_DOWN_GROUP or c + 1 == n_chunks:
                ks = slice(group_start * FFN_COL_CHUNK, (c + 1) * FFN_COL_CHUNK)
                part = _dot(act_ref[rows, ks], wd_ref[ks, :])
                y = part if y is None else y + part
                group_start = c + 1
        o_ref[0, rows, :] = _layer_norm(alpha * hv + y, g_ref[...], b_ref[...])

    @pl.when(t == pl.num_programs(1) - 1)
    def _():
        tail_ref[0] = carry_ref[...]


def _ffn(h, w_up, w_down, layer, conv_w, g, b, halo, alpha, tm, mixer=None):
    bsz, t_len, d = h.shape
    d_ff = conv_w.shape[1]
    operands = [h, w_up, conv_w, w_down, g, b, halo]
    in_specs = [_row_spec(tm, d), _layer_spec(w_up, layer), _const_spec(conv_w.shape),
                _layer_spec(w_down, layer)] + [_const_spec(x.shape) for x in (g, b, halo)]
    if mixer is not None:
        a, w_o, w_o_layer, g1, b1 = mixer
        operands = [a, w_o, g1, b1] + operands
        in_specs = [_row_spec(tm, a.shape[2]), _layer_spec(w_o, w_o_layer),
                    _const_spec(g1.shape), _const_spec(b1.shape)] + in_specs
    return pl.pallas_call(
        functools.partial(_ffn_kernel, alpha=alpha, tm=tm, mixer_tail=mixer is not None),
        grid=(bsz, t_len // tm),
        in_specs=in_specs,
        out_specs=(_row_spec(tm, d), _tail_spec(d_ff)),
        out_shape=(jax.ShapeDtypeStruct((bsz, t_len, d), F32),
                   jax.ShapeDtypeStruct((bsz, HALO, d_ff), F32)),
        scratch_shapes=[pltpu.VMEM((HALO, d_ff), F32), pltpu.VMEM((tm, d_ff), BF16)],
        compiler_params=_params(),
        name="conv_ffn",
    )(*operands)


def _gdn_mixer(hm, hx, w_in_f32, w_in, layer, conv_w, a_log, dt_bias, norm_w):
    w3 = 3 * GDN_WIDTH
    wba = jnp.pad(w_in_f32[:, w3 + GDN_WIDTH:], ((0, 0), (0, LANES - 2 * GDN_HEADS))).astype(BF16)
    lane_pad = (GDN_HEADS, LANES - 2 * GDN_HEADS)
    gscale = jnp.pad(-jnp.exp(a_log.astype(F32)), lane_pad)[None]
    dtb = jnp.pad(dt_bias.astype(F32), lane_pad)[None]
    nw = norm_w.astype(F32)[None]
    state_shape = (GDN_HEADS, GDN_HEAD_DIM, GDN_HEAD_DIM)

    qm, km, vm, zm, scm, tail_m = _gdn_in(hm, w_in, layer, wba, conv_w, gscale, dtb,
                                          jnp.zeros((HALO, w3), F32), N_META)
    front = ((0, 0), (CHUNK - N_META, 0), (0, 0))
    om, state_m = _gdn(jnp.pad(qm, front), jnp.pad(km, front), jnp.pad(vm, front),
                       jnp.pad(zm, front), jnp.pad(scm, front), nw,
                       jnp.zeros(state_shape, F32), CHUNK)

    qx, kx, vx, zx, scx, _ = _gdn_in(hx, w_in, layer, wba, conv_w, gscale, dtb, tail_m[0],
                                     ROW_TILE)
    ox, _ = _gdn(qx, kx, vx, zx, scx, nw, state_m[0], GDN_ROW_TILE)
    return om[:, CHUNK - N_META:], ox


def _sconv_layer(hm, hx, w_in, w_out, layer, conv_w, ln_g, ln_b, alpha):
    g2, b2 = ln_g[None], ln_b[None]
    width = conv_w.shape[1]
    hm_new, tail_m = _sconv(hm, w_in, w_out, layer, conv_w, g2, b2,
                            jnp.zeros((HALO, width), F32), alpha, N_META)
    hx_new, _ = _sconv(hx, w_in, w_out, layer, conv_w, g2, b2, tail_m[0], alpha, ROW_TILE)
    return hm_new, hx_new


def _ffn_layer(hm, hx, w_up, w_down, layer, conv_w, ln_g, ln_b, alpha, mixer=None):
    d_ff = conv_w.shape[1]
    g2, b2 = ln_g[None], ln_b[None]
    mix_m = mix_x = None
    if mixer is not None:
        am, ax, w_o, w_o_layer, g1, b1 = mixer
        tail = (w_o, w_o_layer, g1[None], b1[None])
        mix_m, mix_x = (am,) + tail, (ax,) + tail
    hm_new, tail_m = _ffn(hm, w_up, w_down, layer, conv_w, g2, b2, jnp.zeros((HALO, d_ff), F32),
                          alpha, N_META, mix_m)
    hx_new, _ = _ffn(hx, w_up, w_down, layer, conv_w, g2, b2, tail_m[0], alpha, ROW_TILE, mix_x)
    return hm_new, hx_new


def kernel(x, meta, a_w_in, a_conv, a_log, a_dt_bias, a_norm, a_w_out, b_w_in, b_conv, b_w_out,
           ln_mix_g, ln_mix_b, ffn_w_up, ffn_conv, ffn_w_down, ln_ffn_g, ln_ffn_b):
    depth = ln_mix_g.shape[0]
    alpha = (2.0 * depth) ** 0.25
    assert x.shape[1] % ROW_TILE == 0 and meta.shape[0] == N_META
    hx = x
    hm = meta.astype(x.dtype)[None]
    a_in, a_out = a_w_in.astype(BF16), a_w_out.astype(BF16)
    b_in, b_out = b_w_in.astype(BF16), b_w_out.astype(BF16)
    f_up, f_down = ffn_w_up.astype(BF16), ffn_w_down.astype(BF16)
    for i in range(depth):
        j = i // 2
        mixer = None
        if i % 2 == 0:
            am, ax = _gdn_mixer(hm, hx, a_w_in[j], a_in, j, a_conv[j], a_log[j], a_dt_bias[j],
                                a_norm[j])
            mixer = (am, ax, a_out, j, ln_mix_g[i], ln_mix_b[i])
        else:
            hm, hx = _sconv_layer(hm, hx, b_in, b_out, j, b_conv[j], ln_mix_g[i], ln_mix_b[i],
                                  alpha)
        hm, hx = _ffn_layer(hm, hx, f_up, f_down, i, ffn_conv[i], ln_ffn_g[i], ln_ffn_b[i], alpha,
                            mixer)
    return hx
```

```python
import functools

import jax
import jax.numpy as jnp
from jax import lax
from jax.experimental import pallas as pl
from jax.experimental.pallas import tpu as pltpu

F32 = jnp.float32
BF16 = jnp.bfloat16

N_META = 16
GDN_HEADS = 8
GDN_HEAD_DIM = 128
GDN_WIDTH = GDN_HEADS * GDN_HEAD_DIM
Q_SCALE = GDN_HEAD_DIM ** -0.5
CHUNK = 64
INV_BLOCK = 16
HALO = 8
LANES = 128
LN_EPS = 1e-5
RMS_EPS = 1e-6
L2_EPS = 1e-6
MASK_NEG = -1e30

ROW_TILE = 512
SUB_ROWS = 256
GDN_ROW_TILE = 512
GDN_GROUP = 4
QKV_COL_CHUNK = 512
EW_ROWS = 64
MXU_AHEAD = 2
QKV_SLOTS = MXU_AHEAD + 1
FFN_COL_CHUNK = 256
FFN_DOWN_GROUP = 4
SCONV_COL_CHUNK = 256
VMEM_LIMIT = 56 * 1024 * 1024


def _dot(a, b):
    return jnp.dot(a, b, preferred_element_type=F32)


def _dot_nt(a, b):
    return lax.dot_general(a, b, (((1,), (1,)), ((), ())), preferred_element_type=F32)


def _sigmoid(x):
    return 1.0 / (1.0 + jnp.exp(-x))


def _layer_norm(x, g, b):
    mu = jnp.mean(x, axis=-1, keepdims=True)
    xc = x - mu
    var = jnp.mean(xc * xc, axis=-1, keepdims=True)
    return xc * lax.rsqrt(var + LN_EPS) * g + b


def _const_spec(shape):
    nd = len(shape)
    return pl.BlockSpec(shape, lambda *_, _nd=nd: (0,) * _nd, pipeline_mode=pl.Buffered(1))


def _layer_spec(stacked, layer):
    nd = stacked.ndim - 1
    return pl.BlockSpec((None,) + stacked.shape[1:], lambda *_, _nd=nd: (layer,) + (0,) * _nd,
                        pipeline_mode=pl.Buffered(1))


def _row_spec(tm, width):
    return pl.BlockSpec((1, tm, width), lambda b, t: (b, t, 0))


def _tail_spec(width):
    return pl.BlockSpec((1, HALO, width), lambda b, t: (b, 0, 0))


def _params():
    return pltpu.CompilerParams(dimension_semantics=("arbitrary", "arbitrary"),
                                vmem_limit_bytes=VMEM_LIMIT)


def _spread_taps(taps):
    return [jnp.broadcast_to(taps[j:j + 1, :], (HALO, taps.shape[1]))[None]
            for j in range(taps.shape[0])]


def _causal_taps(carry, cur, taps):
    width = len(taps)
    tm, c = cur.shape
    assert width - 1 <= HALO
    tiles = jnp.concatenate([carry, cur], axis=0).reshape(tm // HALO + 1, HALO, c)
    first = lax.broadcasted_iota(jnp.int32, (tm // HALO, HALO, c), 1) == 0
    acc = tiles * taps[0]
    for j in range(1, width):
        rot = pltpu.roll(acc, 1, 1)
        shifted = jnp.where(first, rot[:-1], rot[1:])
        if j < width - 1:
            acc = tiles * taps[j] + jnp.concatenate([rot[:1], shifted], axis=0)
        else:
            acc = tiles[1:] * taps[j] + shifted
    return acc.reshape(tm, c)


def _gdn_in_kernel(h_ref, w_ref, wba_ref, cw_ref, gs_ref, dtb_ref, halo_ref,
                   q_ref, k_ref, v_ref, z_ref, sc_ref, tail_ref, st_ref, carry_ref, *, tm):
    t = pl.program_id(1)

    @pl.when(t == 0)
    def _():
        carry_ref[...] = halo_ref[...]

    hb = h_ref[0].astype(BF16)
    outs = (q_ref, k_ref, v_ref)
    heads_per_chunk = QKV_COL_CHUNK // GDN_HEAD_DIM
    chunks_per_out = GDN_WIDTH // QKV_COL_CHUNK
    n_chunks = 3 * chunks_per_out
    col_slice = lambda c: slice(c * QKV_COL_CHUNK, (c + 1) * QKV_COL_CHUNK)
    rb = min(EW_ROWS, tm)

    def project(c):
        st_ref[c % QKV_SLOTS, HALO:HALO + tm, :] = _dot(hb, w_ref[:, col_slice(c)])

    def project_z(i):
        zs = slice(i * QKV_COL_CHUNK, (i + 1) * QKV_COL_CHUNK)
        ws = slice(3 * GDN_WIDTH + i * QKV_COL_CHUNK, 3 * GDN_WIDTH + (i + 1) * QKV_COL_CHUNK)
        z_ref[0, :, zs] = _dot(hb, w_ref[:, ws]).astype(z_ref.dtype)

    for c in range(min(MXU_AHEAD, n_chunks)):
        project(c)
    ba = None
    for c in range(n_chunks):
        cs = col_slice(c)
        st = st_ref.at[c % QKV_SLOTS]
        if c + MXU_AHEAD < n_chunks:
            project(c + MXU_AHEAD)
        if c < chunks_per_out:
            project_z(c)
        elif ba is None:
            ba = _dot(hb, wba_ref[...])
        st[0:HALO, :] = carry_ref[:, cs]
        carry_ref[:, cs] = st[tm:tm + HALO, :]
        which = c // chunks_per_out
        for hh in range(heads_per_chunk):
            hs = slice(hh * GDN_HEAD_DIM, (hh + 1) * GDN_HEAD_DIM)
            col = (c % chunks_per_out) * QKV_COL_CHUNK + hh * GDN_HEAD_DIM
            taps = _spread_taps(cw_ref[:, c * QKV_COL_CHUNK + hh * GDN_HEAD_DIM:
                                       c * QKV_COL_CHUNK + (hh + 1) * GDN_HEAD_DIM])
            for r0 in range(0, tm, rb):
                blk = st[r0:r0 + rb + HALO, hs]
                y = _causal_taps(blk[:HALO], blk[HALO:], taps)
                y = y * _sigmoid(y)
                if which < 2:
                    inv = lax.rsqrt(jnp.sum(y * y, axis=-1, keepdims=True) + L2_EPS)
                    y = y * (inv * Q_SCALE if which == 0 else inv)
                outs[which][0, r0:r0 + rb, col:col + GDN_HEAD_DIM] = y.astype(outs[which].dtype)

    lane = lax.broadcasted_iota(jnp.int32, ba.shape, 1)
    xg = ba + dtb_ref[...]
    softplus = jnp.maximum(xg, 0.0) + jnp.log(1.0 + jnp.exp(-jnp.abs(xg)))
    sc_ref[0] = jnp.where(lane < GDN_HEADS, _sigmoid(ba), gs_ref[...] * softplus)

    @pl.when(t == pl.num_programs(1) - 1)
    def _():
        tail_ref[0] = carry_ref[...]


def _gdn_in(h, w_in, layer, wba, conv_w, gscale, dtb, halo, tm):
    bsz, t_len, d = h.shape
    w3 = 3 * GDN_WIDTH
    row = lambda width: _row_spec(tm, width)
    out_shape = (
        jax.ShapeDtypeStruct((bsz, t_len, GDN_WIDTH), F32),
        jax.ShapeDtypeStruct((bsz, t_len, GDN_WIDTH), F32),
        jax.ShapeDtypeStruct((bsz, t_len, GDN_WIDTH), F32),
        jax.ShapeDtypeStruct((bsz, t_len, GDN_WIDTH), F32),
        jax.ShapeDtypeStruct((bsz, t_len, LANES), F32),
        jax.ShapeDtypeStruct((bsz, HALO, w3), F32),
    )
    return pl.pallas_call(
        functools.partial(_gdn_in_kernel, tm=tm),
        grid=(bsz, t_len // tm),
        in_specs=[row(d), _layer_spec(w_in, layer), _const_spec(wba.shape),
                  _const_spec(conv_w.shape), _const_spec(gscale.shape), _const_spec(dtb.shape),
                  _const_spec(halo.shape)],
        out_specs=(row(GDN_WIDTH), row(GDN_WIDTH), row(GDN_WIDTH), row(GDN_WIDTH), row(LANES),
                   _tail_spec(w3)),
        out_shape=out_shape,
        scratch_shapes=[pltpu.VMEM((QKV_SLOTS, tm + HALO, QKV_COL_CHUNK), F32),
                        pltpu.VMEM((HALO, w3), F32)],
        compiler_params=_params(),
        name="gdn_in",
    )(h, w_in, wba, conv_w, gscale, dtb, halo)


def _block_diag(x):
    half = x.shape[1] // 2
    zero = jnp.zeros((x.shape[0], half), x.dtype)
    return jnp.concatenate([jnp.concatenate([x[:, :half], zero], axis=1),
                            jnp.concatenate([zero, x[:, half:]], axis=1)], axis=0)


def _mm_pairs(a_list, b_list):
    return [_dot(a.astype(BF16), _block_diag(b.astype(BF16))) for a, b in zip(a_list, b_list)]


def _unit_lower_inverse(ms, eye, blockdiag, out):
    d = [jnp.where(blockdiag, m, 0.0) for m in ms]
    lo = [jnp.where(blockdiag, 0.0, m) for m in ms]
    d2 = _mm_pairs(d, d)
    yield
    d4, p = _square_and_times(d2, [eye - x for x in d], eye)
    yield
    d8, p = _square_and_times(d4, p, eye)
    yield
    p = _mm_pairs(p, [eye + x for x in d8])
    yield
    n = _mm_pairs(p, lo)
    yield
    n2 = _mm_pairs(n, n)
    yield
    qn = _mm_pairs([eye - x for x in n], [eye + x for x in n2])
    yield
    out[:] = _mm_pairs(qn, p)
    yield


def _square_and_times(xs, ps, eye):
    xb = [x.astype(BF16) for x in xs]
    both = _mm_pairs([jnp.concatenate([x, p.astype(BF16)], axis=0) for x, p in zip(xb, ps)],
                     [eye + x for x in xs])
    return ([r[:CHUNK] - x.astype(F32) for r, x in zip(both, xb)], [r[CHUNK:] for r in both])


def _interleave(major, minor, n_major, n_minor):
    done = 0
    for i, _ in enumerate(major):
        want = ((i + 1) * n_minor) // n_major
        while done < want and next(minor, StopIteration) is not StopIteration:
            done += 1
    for _ in minor:
        pass


PHASE1_STEPS = 11
PHASE2_STEPS_PER_CHUNK = 2


def _gdn_kernel(q_ref, k_ref, v_ref, z_ref, sc_ref, nw_ref, s0_ref, o_ref, sfin_ref, state_ref,
                *, n_chunks):
    t = pl.program_id(1)

    @pl.when(t == 0)
    def _():
        state_ref[...] = s0_ref[...]

    dh = GDN_HEAD_DIM
    ri = lax.broadcasted_iota(jnp.int32, (CHUNK, 2 * CHUNK), 0)
    li = lax.broadcasted_iota(jnp.int32, (CHUNK, 2 * CHUNK), 1)
    left = li < CHUNK
    ci = jnp.where(left, li, li - CHUNK)
    causal = ri >= ci
    strict = ri > ci
    blockdiag = (ri // INV_BLOCK) == (ci // INV_BLOCK)
    eye = jnp.where(ri == ci, 1.0, 0.0).astype(F32)
    ri1 = lax.broadcasted_iota(jnp.int32, (CHUNK, CHUNK), 0)
    ci1 = lax.broadcasted_iota(jnp.int32, (CHUNK, CHUNK), 1)
    tril_b = jnp.where(ri1 >= ci1, 1.0, 0.0).astype(BF16)
    nw = nw_ref[...]

    def per_head(col_a, col_b):
        return jnp.concatenate([jnp.broadcast_to(col_a, (CHUNK, dh)),
                                jnp.broadcast_to(col_b, (CHUNK, dh))], axis=1)

    pairs = range(GDN_HEADS // 2)
    rows = [slice(c * CHUNK, (c + 1) * CHUNK) for c in range(n_chunks)]
    pcols = [slice(2 * j * dh, (2 * j + 2) * dh) for j in pairs]

    def col(x, lane):
        return x[:, lane:lane + 1]

    def phase1(chunks, res):
        items = [(c, j) for c in chunks for j in pairs]
        scc, gc, gct2 = {}, {}, {}
        for c in chunks:
            scc[c] = sc_ref[0, rows[c], :]
            hi = scc[c].astype(BF16)
            r1 = scc[c] - hi.astype(F32)
            mid = r1.astype(BF16)
            low = (r1 - mid.astype(F32)).astype(BF16)
            gc[c] = _dot(tril_b, hi) + _dot(tril_b, mid) + _dot(tril_b, low)
            gct2[c] = jnp.concatenate([gc[c], gc[c]], axis=0).T
        yield
        q = [q_ref[0, rows[c], pcols[j]].astype(F32) for c, j in items]
        k = [k_ref[0, rows[c], pcols[j]].astype(F32) for c, j in items]
        v = [v_ref[0, rows[c], pcols[j]].astype(F32) for c, j in items]
        g_a = [col(gc[c], GDN_HEADS + 2 * j) for c, j in items]
        g_b = [col(gc[c], GDN_HEADS + 2 * j + 1) for c, j in items]
        gl_a = [x[CHUNK - 1:CHUNK, :] for x in g_a]
        gl_b = [x[CHUNK - 1:CHUNK, :] for x in g_b]
        beta = [per_head(col(scc[c], 2 * j), col(scc[c], 2 * j + 1)) for c, j in items]
        eg = [per_head(jnp.exp(a), jnp.exp(b)) for a, b in zip(g_a, g_b)]
        kd_scale = [per_head(jnp.exp(la - a), jnp.exp(lb - b))
                    for a, b, la, lb in zip(g_a, g_b, gl_a, gl_b)]
        gcol = [jnp.where(left, a, b) for a, b in zip(g_a, g_b)]
        grow = [jnp.where(left[:1], gct2[c][GDN_HEADS + 2 * j:GDN_HEADS + 2 * j + 1, :],
                          gct2[c][GDN_HEADS + 2 * j + 1:GDN_HEADS + 2 * j + 2, :])
                for c, j in items]
        decay = [jnp.exp(jnp.where(causal, gc_ - gr_, MASK_NEG)) for gc_, gr_ in zip(gcol, grow)]
        kb = [k_ * b_ for k_, b_ in zip(k, beta)]
        kk_qk = [_dot_nt(jnp.concatenate([kb_, q_], axis=0).astype(BF16),
                         _block_diag(k_.astype(BF16))) for kb_, q_, k_ in zip(kb, q, k)]
        yield
        m = [jnp.where(strict, x[:CHUNK] * d_, 0.0) for x, d_ in zip(kk_qk, decay)]
        qk = [x[CHUNK:] * d_ for x, d_ in zip(kk_qk, decay)]
        tinv = []
        yield from _unit_lower_inverse(m, eye, blockdiag, tinv)
        vb = [v_ * b_ for v_, b_ in zip(v, beta)]
        kbg = [kb_ * e_ for kb_, e_ in zip(kb, eg)]
        uw = [_dot(_block_diag(t_.astype(BF16)),
                   jnp.concatenate([jnp.concatenate([vb_[:, :dh], kbg_[:, :dh]], axis=1),
                                    jnp.concatenate([vb_[:, dh:], kbg_[:, dh:]], axis=1)],
                                   axis=0).astype(BF16))
              for t_, vb_, kbg_ in zip(tinv, vb, kbg)]
        for i, item in enumerate(items):
            kd = k[i] * kd_scale[i]
            kdt = jnp.concatenate([kd[:, :dh], kd[:, dh:]], axis=0).T
            res[item] = dict(
                u=(uw[i][:CHUNK, :dh], uw[i][CHUNK:, :dh]),
                lhs1=jnp.concatenate([jnp.concatenate([uw[i][:CHUNK, dh:], uw[i][CHUNK:, dh:]],
                                                      axis=1), q[i] * eg[i]], axis=0).astype(BF16),
                lhs2=jnp.concatenate([qk[i], kdt], axis=0).astype(BF16),
                sdec=(jnp.exp(gl_a[i]), jnp.exp(gl_b[i])))
        yield

    def phase2(chunks, res):
        for c in chunks:
            r = [res[c, j] for j in pairs]
            s = [(state_ref[2 * j], state_ref[2 * j + 1]) for j in pairs]
            ws_qs = [_dot(x["lhs1"], _block_diag(jnp.concatenate(s_, axis=1).astype(BF16)))
                     for x, s_ in zip(r, s)]
            yield
            v_new = [jnp.concatenate([x["u"][0] - y[:CHUNK, :dh], x["u"][1] - y[:CHUNK, dh:]],
                                     axis=1) for x, y in zip(r, ws_qs)]
            r2 = [_dot(x["lhs2"], _block_diag(vn.astype(BF16))) for x, vn in zip(r, v_new)]
            for j in pairs:
                state_ref[2 * j] = s[j][0] * r[j]["sdec"][0] + r2[j][CHUNK:, :dh]
                state_ref[2 * j + 1] = s[j][1] * r[j]["sdec"][1] + r2[j][CHUNK:, dh:]
            yield
            for j in pairs:
                o2 = ws_qs[j][CHUNK:] + r2[j][:CHUNK]
                for half in range(2):
                    o = o2[:, half * dh:(half + 1) * dh]
                    o = o * lax.rsqrt(jnp.mean(o * o, axis=-1, keepdims=True) + RMS_EPS) * nw
                    cs = slice((2 * j + half) * dh, (2 * j + half + 1) * dh)
                    zh = z_ref[0, rows[c], cs].astype(F32)
                    o_ref[0, rows[c], cs] = (o * (zh * _sigmoid(zh))).astype(o_ref.dtype)

    groups = [list(range(g, min(g + GDN_GROUP, n_chunks))) for g in range(0, n_chunks, GDN_GROUP)]
    res = {}
    for _ in phase1(groups[0], res):
        pass
    for prev, cur in zip(groups[:-1], groups[1:]):
        _interleave(phase1(cur, res), phase2(prev, res), PHASE1_STEPS,
                    PHASE2_STEPS_PER_CHUNK * len(prev))
    for _ in phase2(groups[-1], res):
        pass

    @pl.when(t == pl.num_programs(1) - 1)
    def _():
        sfin_ref[0] = state_ref[...]


def _gdn(q, k, v, z, sc, norm_w, state0, tm):
    bsz, t_len, _ = q.shape
    row = lambda width: _row_spec(tm, width)
    state_shape = (GDN_HEADS, GDN_HEAD_DIM, GDN_HEAD_DIM)
    return pl.pallas_call(
        functools.partial(_gdn_kernel, n_chunks=tm // CHUNK),
        grid=(bsz, t_len // tm),
        in_specs=[row(GDN_WIDTH), row(GDN_WIDTH), row(GDN_WIDTH), row(GDN_WIDTH), row(LANES),
                  _const_spec(norm_w.shape), _const_spec(state0.shape)],
        out_specs=(row(GDN_WIDTH),
                   pl.BlockSpec((1,) + state_shape, lambda b, t: (b, 0, 0, 0))),
        out_shape=(jax.ShapeDtypeStruct((bsz, t_len, GDN_WIDTH), BF16),
                   jax.ShapeDtypeStruct((bsz,) + state_shape, F32)),
        scratch_shapes=[pltpu.VMEM(state_shape, F32)],
        compiler_params=_params(),
        name="gdn_scan",
    )(q, k, v, z, sc, norm_w, state0)


def _sconv_kernel(h_ref, win_ref, cw_ref, wout_ref, g_ref, b_ref, halo_ref, o_ref, tail_ref,
                  carry_ref, act_ref, *, alpha, tm):
    t = pl.program_id(1)

    @pl.when(t == 0)
    def _():
        carry_ref[...] = halo_ref[...]

    width = cw_ref.shape[1]
    n_chunks = width // SCONV_COL_CHUNK
    rs = min(SUB_ROWS, tm)

    for r0 in range(0, tm, rs):
        rows = slice(r0, r0 + rs)
        hv = h_ref[0, rows, :]
        hb = hv.astype(BF16)

        def proj(c):
            lo = c * SCONV_COL_CHUNK
            return [_dot(hb, win_ref[:, part * width + lo:part * width + lo + SCONV_COL_CHUNK])
                    for part in range(3)]

        nxt = proj(0)
        for c in range(n_chunks):
            cs = slice(c * SCONV_COL_CHUNK, (c + 1) * SCONV_COL_CHUNK)
            b_gate, c_gate, xv = nxt
            if c + 1 < n_chunks:
                nxt = proj(c + 1)
            cx = c_gate * xv
            u = _causal_taps(carry_ref[:, cs], cx, _spread_taps(cw_ref[:, cs]))
            carry_ref[:, cs] = cx[rs - HALO:rs, :]
            act_ref[rows, cs] = (b_gate * u).astype(BF16)
        y = _dot(act_ref[rows, :], wout_ref[...])
        o_ref[0, rows, :] = _layer_norm(alpha * hv + y, g_ref[...], b_ref[...])

    @pl.when(t == pl.num_programs(1) - 1)
    def _():
        tail_ref[0] = carry_ref[...]


def _sconv(h, w_in, w_out, layer, conv_w, g, b, halo, alpha, tm):
    bsz, t_len, d = h.shape
    width = conv_w.shape[1]
    return pl.pallas_call(
        functools.partial(_sconv_kernel, alpha=alpha, tm=tm),
        grid=(bsz, t_len // tm),
        in_specs=[_row_spec(tm, d), _layer_spec(w_in, layer), _const_spec(conv_w.shape),
                  _layer_spec(w_out, layer), _const_spec(g.shape), _const_spec(b.shape),
                  _const_spec(halo.shape)],
        out_specs=(_row_spec(tm, d), _tail_spec(width)),
        out_shape=(jax.ShapeDtypeStruct((bsz, t_len, d), F32),
                   jax.ShapeDtypeStruct((bsz, HALO, width), F32)),
        scratch_shapes=[pltpu.VMEM((HALO, width), F32), pltpu.VMEM((tm, width), BF16)],
        compiler_params=_params(),
        name="sconv_mixer",
    )(h, w_in, conv_w, w_out, g, b, halo)


def _ffn_kernel(*refs, alpha, tm, mixer_tail):
    if mixer_tail:
        a_ref, wo_ref, g1_ref, b1_ref = refs[:4]
        refs = refs[4:]
    (h_ref, wup_ref, cw_ref, wd_ref, g_ref, b_ref, halo_ref, o_ref, tail_ref,
     carry_ref, act_ref) = refs
    t = pl.program_id(1)

    @pl.when(t == 0)
    def _():
        carry_ref[...] = halo_ref[...]

    d_ff = cw_ref.shape[1]
    n_chunks = d_ff // FFN_COL_CHUNK
    col_slice = lambda c: slice(c * FFN_COL_CHUNK, (c + 1) * FFN_COL_CHUNK)
    rs = min(SUB_ROWS, tm)

    subs = [slice(r0, r0 + rs) for r0 in range(0, tm, rs)]
    if mixer_tail:
        proj = [_dot(a_ref[0, rows, :], wo_ref[...]) for rows in subs]
        for rows, y in zip(subs, proj):
            o_ref[0, rows, :] = _layer_norm(alpha * h_ref[0, rows, :] + y, g1_ref[...], b1_ref[...])
    x_ref = o_ref if mixer_tail else h_ref
    for rows in subs:
        hv = x_ref[0, rows, :]
        hb = hv.astype(BF16)
        up = lambda c: (_dot(hb, wup_ref[:, col_slice(c)]),
                        _dot(hb, wup_ref[:, d_ff + c * FFN_COL_CHUNK:d_ff + (c + 1) * FFN_COL_CHUNK]))
        nxt = up(0)
        y = None
        group_start = 0
        for c in range(n_chunks):
            cs = col_slice(c)
            u, gate = nxt
            if c + 1 < n_chunks:
                nxt = up(c + 1)
            uc = _causal_taps(carry_ref[:, cs], u, _spread_taps(cw_ref[:, cs]))
            carry_ref[:, cs] = u[rs - HALO:rs, :]
            act_ref[rows, cs] = (uc * _sigmoid(uc) * gate).astype(BF16)
            if (c + 1 - group_start) == FFN_DOWN_GROUP or c + 1 == n_chunks:
                ks = slice(group_start * FFN_COL_CHUNK, (c + 1) * FFN_COL_CHUNK)
                part = _dot(act_ref[rows, ks], wd_ref[ks, :])
                y = part if y is None else y + part
                group_start = c + 1
        o_ref[0, rows, :] = _layer_norm(alpha * hv + y, g_ref[...], b_ref[...])

    @pl.when(t == pl.num_programs(1) - 1)
    def _():
        tail_ref[0] = carry_ref[...]


def _ffn(h, w_up, w_down, layer, conv_w, g, b, halo, alpha, tm, mixer=None):
    bsz, t_len, d = h.shape
    d_ff = conv_w.shape[1]
    operands = [h, w_up, conv_w, w_down, g, b, halo]
    in_specs = [_row_spec(tm, d), _layer_spec(w_up, layer), _const_spec(conv_w.shape),
                _layer_spec(w_down, layer)] + [_const_spec(x.shape) for x in (g, b, halo)]
    if mixer is not None:
        a, w_o, w_o_layer, g1, b1 = mixer
        operands = [a, w_o, g1, b1] + operands
        in_specs = [_row_spec(tm, a.shape[2]), _layer_spec(w_o, w_o_layer),
                    _const_spec(g1.shape), _const_spec(b1.shape)] + in_specs
    return pl.pallas_call(
        functools.partial(_ffn_kernel, alpha=alpha, tm=tm, mixer_tail=mixer is not None),
        grid=(bsz, t_len // tm),
        in_specs=in_specs,
        out_specs=(_row_spec(tm, d), _tail_spec(d_ff)),
        out_shape=(jax.ShapeDtypeStruct((bsz, t_len, d), F32),
                   jax.ShapeDtypeStruct((bsz, HALO, d_ff), F32)),
        scratch_shapes=[pltpu.VMEM((HALO, d_ff), F32), pltpu.VMEM((tm, d_ff), BF16)],
        compiler_params=_params(),
        name="conv_ffn",
    )(*operands)


def _gdn_mixer(hm, hx, w_in_f32, w_in, layer, conv_w, a_log, dt_bias, norm_w):
    w3 = 3 * GDN_WIDTH
    wba = jnp.pad(w_in_f32[:, w3 + GDN_WIDTH:], ((0, 0), (0, LANES - 2 * GDN_HEADS))).astype(BF16)
    lane_pad = (GDN_HEADS, LANES - 2 * GDN_HEADS)
    gscale = jnp.pad(-jnp.exp(a_log.astype(F32)), lane_pad)[None]
    dtb = jnp.pad(dt_bias.astype(F32), lane_pad)[None]
    nw = norm_w.astype(F32)[None]
    state_shape = (GDN_HEADS, GDN_HEAD_DIM, GDN_HEAD_DIM)

    qm, km, vm, zm, scm, tail_m = _gdn_in(hm, w_in, layer, wba, conv_w, gscale, dtb,
                                          jnp.zeros((HALO, w3), F32), N_META)
    front = ((0, 0), (CHUNK - N_META, 0), (0, 0))
    om, state_m = _gdn(jnp.pad(qm, front), jnp.pad(km, front), jnp.pad(vm, front),
                       jnp.pad(zm, front), jnp.pad(scm, front), nw,
                       jnp.zeros(state_shape, F32), CHUNK)

    qx, kx, vx, zx, scx, _ = _gdn_in(hx, w_in, layer, wba, conv_w, gscale, dtb, tail_m[0],
                                     ROW_TILE)
    ox, _ = _gdn(qx, kx, vx, zx, scx, nw, state_m[0], GDN_ROW_TILE)
    return om[:, CHUNK - N_META:], ox


def _sconv_layer(hm, hx, w_in, w_out, layer, conv_w, ln_g, ln_b, alpha):
    g2, b2 = ln_g[None], ln_b[None]
    width = conv_w.shape[1]
    hm_new, tail_m = _sconv(hm, w_in, w_out, layer, conv_w, g2, b2,
                            jnp.zeros((HALO, width), F32), alpha, N_META)
    hx_new, _ = _sconv(hx, w_in, w_out, layer, conv_w, g2, b2, tail_m[0], alpha, ROW_TILE)
    return hm_new, hx_new


def _ffn_layer(hm, hx, w_up, w_down, layer, conv_w, ln_g, ln_b, alpha, mixer=None):
    d_ff = conv_w.shape[1]
    g2, b2 = ln_g[None], ln_b[None]
    mix_m = mix_x = None
    if mixer is not None:
        am, ax, w_o, w_o_layer, g1, b1 = mixer
        tail = (w_o, w_o_layer, g1[None], b1[None])
        mix_m, mix_x = (am,) + tail, (ax,) + tail
    hm_new, tail_m = _ffn(hm, w_up, w_down, layer, conv_w, g2, b2, jnp.zeros((HALO, d_ff), F32),
                          alpha, N_META, mix_m)
    hx_new, _ = _ffn(hx, w_up, w_down, layer, conv_w, g2, b2, tail_m[0], alpha, ROW_TILE, mix_x)
    return hm_new, hx_new


def kernel(x, meta, a_w_in, a_conv, a_log, a_dt_bias, a_norm, a_w_out, b_w_in, b_conv, b_w_out,
           ln_mix_g, ln_mix_b, ffn_w_up, ffn_conv, ffn_w_down, ln_ffn_g, ln_ffn_b):
    depth = ln_mix_g.shape[0]
    alpha = (2.0 * depth) ** 0.25
    assert x.shape[1] % ROW_TILE == 0 and meta.shape[0] == N_META
    hx = x
    hm = meta.astype(x.dtype)[None]
    a_in, a_out = a_w_in.astype(BF16), a_w_out.astype(BF16)
    b_in, b_out = b_w_in.astype(BF16), b_w_out.astype(BF16)
    f_up, f_down = ffn_w_up.astype(BF16), ffn_w_down.astype(BF16)
    for i in range(depth):
        j = i // 2
        mixer = None
        if i % 2 == 0:
            am, ax = _gdn_mixer(hm, hx, a_w_in[j], a_in, j, a_conv[j], a_log[j], a_dt_bias[j],
                                a_norm[j])
            mixer = (am, ax, a_out, j, ln_mix_g[i], ln_mix_b[i])
        else:
            hm, hx = _sconv_layer(hm, hx, b_in, b_out, j, b_conv[j], ln_mix_g[i], ln_mix_b[i],
                                  alpha)
        hm, hx = _ffn_layer(hm, hx, f_up, f_down, i, ffn_conv[i], ln_ffn_g[i], ln_ffn_b[i], alpha,
                            mixer)
    return hx
```

```python
import functools

import jax
import jax.numpy as jnp
from jax import lax
from jax.experimental import pallas as pl
from jax.experimental.pallas import tpu as pltpu

F32 = jnp.float32
BF16 = jnp.bfloat16

N_META = 16
GDN_HEADS = 8
GDN_HEAD_DIM = 128
GDN_WIDTH = GDN_HEADS * GDN_HEAD_DIM
Q_SCALE = GDN_HEAD_DIM ** -0.5
CHUNK = 64
INV_BLOCK = 16
HALO = 8
LANES = 128
Q_COL, K_COL, V_COL, Z_COL, SC_COL = (i * GDN_WIDTH for i in range(5))
ACT_WIDTH = SC_COL + LANES
LN_EPS = 1e-5
RMS_EPS = 1e-6
L2_EPS = 1e-6
MASK_NEG = -1e30

ROW_TILE = 512
SUB_ROWS = 256
GDN_ROW_TILE = 512
GDN_GROUP = 4
QKV_COL_CHUNK = 512
EW_ROWS = 64
MXU_AHEAD = 2
QKV_SLOTS = MXU_AHEAD + 1
FFN_COL_CHUNK = 256
FFN_DOWN_GROUP = 4
SCONV_COL_CHUNK = 256
VMEM_LIMIT = 56 * 1024 * 1024


def _dot(a, b):
    return jnp.dot(a, b, preferred_element_type=F32)


def _dot_nt(a, b):
    return lax.dot_general(a, b, (((1,), (1,)), ((), ())), preferred_element_type=F32)


def _sigmoid(x):
    return 1.0 / (1.0 + jnp.exp(-x))


def _layer_norm(x, g, b):
    mu = jnp.mean(x, axis=-1, keepdims=True)
    xc = x - mu
    var = jnp.mean(xc * xc, axis=-1, keepdims=True)
    return xc * lax.rsqrt(var + LN_EPS) * g + b


def _const_spec(shape):
    nd = len(shape)
    return pl.BlockSpec(shape, lambda *_, _nd=nd: (0,) * _nd, pipeline_mode=pl.Buffered(1))


def _layer_spec(stacked, layer):
    nd = stacked.ndim - 1
    return pl.BlockSpec((None,) + stacked.shape[1:], lambda *_, _nd=nd: (layer,) + (0,) * _nd,
                        pipeline_mode=pl.Buffered(1))


def _row_spec(tm, width):
    return pl.BlockSpec((1, tm, width), lambda b, t: (b, t, 0))


def _tail_spec(width):
    return pl.BlockSpec((1, HALO, width), lambda b, t: (b, 0, 0))


def _params():
    return pltpu.CompilerParams(dimension_semantics=("arbitrary", "arbitrary"),
                                vmem_limit_bytes=VMEM_LIMIT)


def _spread_taps(taps):
    return [jnp.broadcast_to(taps[j:j + 1, :], (HALO, taps.shape[1]))[None]
            for j in range(taps.shape[0])]


def _causal_taps(carry, cur, taps):
    width = len(taps)
    tm, c = cur.shape
    assert width - 1 <= HALO
    tiles = jnp.concatenate([carry, cur], axis=0).reshape(tm // HALO + 1, HALO, c)
    first = lax.broadcasted_iota(jnp.int32, (tm // HALO, HALO, c), 1) == 0
    acc = tiles * taps[0]
    for j in range(1, width):
        rot = pltpu.roll(acc, 1, 1)
        shifted = jnp.where(first, rot[:-1], rot[1:])
        if j < width - 1:
            acc = tiles * taps[j] + jnp.concatenate([rot[:1], shifted], axis=0)
        else:
            acc = tiles[1:] * taps[j] + shifted
    return acc.reshape(tm, c)


def _gdn_in_kernel(h_ref, w_ref, wba_ref, cw_ref, gs_ref, dtb_ref, halo_ref,
                   act_ref, tail_ref, st_ref, carry_ref, *, tm):
    t = pl.program_id(1)

    @pl.when(t == 0)
    def _():
        carry_ref[...] = halo_ref[...]

    hb = h_ref[0].astype(BF16)
    out_col = (Q_COL, K_COL, V_COL)
    heads_per_chunk = QKV_COL_CHUNK // GDN_HEAD_DIM
    chunks_per_out = GDN_WIDTH // QKV_COL_CHUNK
    n_chunks = 3 * chunks_per_out
    col_slice = lambda c: slice(c * QKV_COL_CHUNK, (c + 1) * QKV_COL_CHUNK)
    rb = min(EW_ROWS, tm)

    def project(c):
        st_ref[c % QKV_SLOTS, HALO:HALO + tm, :] = _dot(hb, w_ref[:, col_slice(c)])

    def project_z(i):
        zs = slice(Z_COL + i * QKV_COL_CHUNK, Z_COL + (i + 1) * QKV_COL_CHUNK)
        ws = slice(3 * GDN_WIDTH + i * QKV_COL_CHUNK, 3 * GDN_WIDTH + (i + 1) * QKV_COL_CHUNK)
        act_ref[0, :, zs] = _dot(hb, w_ref[:, ws])

    for c in range(min(MXU_AHEAD, n_chunks)):
        project(c)
    ba = None
    for c in range(n_chunks):
        cs = col_slice(c)
        st = st_ref.at[c % QKV_SLOTS]
        if c + MXU_AHEAD < n_chunks:
            project(c + MXU_AHEAD)
        if c < chunks_per_out:
            project_z(c)
        elif ba is None:
            ba = _dot(hb, wba_ref[...])
        st[0:HALO, :] = carry_ref[:, cs]
        carry_ref[:, cs] = st[tm:tm + HALO, :]
        which = c // chunks_per_out
        for hh in range(heads_per_chunk):
            hs = slice(hh * GDN_HEAD_DIM, (hh + 1) * GDN_HEAD_DIM)
            col = (c % chunks_per_out) * QKV_COL_CHUNK + hh * GDN_HEAD_DIM
            taps = _spread_taps(cw_ref[:, c * QKV_COL_CHUNK + hh * GDN_HEAD_DIM:
                                       c * QKV_COL_CHUNK + (hh + 1) * GDN_HEAD_DIM])
            for r0 in range(0, tm, rb):
                blk = st[r0:r0 + rb + HALO, hs]
                y = _causal_taps(blk[:HALO], blk[HALO:], taps)
                y = y * _sigmoid(y)
                if which < 2:
                    inv = lax.rsqrt(jnp.sum(y * y, axis=-1, keepdims=True) + L2_EPS)
                    y = y * (inv * Q_SCALE if which == 0 else inv)
                act_ref[0, r0:r0 + rb, out_col[which] + col:out_col[which] + col + GDN_HEAD_DIM] = y

    lane = lax.broadcasted_iota(jnp.int32, ba.shape, 1)
    xg = ba + dtb_ref[...]
    softplus = jnp.maximum(xg, 0.0) + jnp.log(1.0 + jnp.exp(-jnp.abs(xg)))
    act_ref[0, :, SC_COL:] = jnp.where(lane < GDN_HEADS, _sigmoid(ba), gs_ref[...] * softplus)

    @pl.when(t == pl.num_programs(1) - 1)
    def _():
        tail_ref[0] = carry_ref[...]


def _gdn_in(h, w_in, layer, wba, conv_w, gscale, dtb, halo, tm):
    bsz, t_len, d = h.shape
    w3 = 3 * GDN_WIDTH
    row = lambda width: _row_spec(tm, width)
    out_shape = (
        jax.ShapeDtypeStruct((bsz, t_len, ACT_WIDTH), F32),
        jax.ShapeDtypeStruct((bsz, HALO, w3), F32),
    )
    return pl.pallas_call(
        functools.partial(_gdn_in_kernel, tm=tm),
        grid=(bsz, t_len // tm),
        in_specs=[row(d), _layer_spec(w_in, layer), _const_spec(wba.shape),
                  _const_spec(conv_w.shape), _const_spec(gscale.shape), _const_spec(dtb.shape),
                  _const_spec(halo.shape)],
        out_specs=(row(ACT_WIDTH), _tail_spec(w3)),
        out_shape=out_shape,
        scratch_shapes=[pltpu.VMEM((QKV_SLOTS, tm + HALO, QKV_COL_CHUNK), F32),
                        pltpu.VMEM((HALO, w3), F32)],
        compiler_params=_params(),
        name="gdn_in",
    )(h, w_in, wba, conv_w, gscale, dtb, halo)


def _block_diag(x):
    half = x.shape[1] // 2
    zero = jnp.zeros((x.shape[0], half), x.dtype)
    return jnp.concatenate([jnp.concatenate([x[:, :half], zero], axis=1),
                            jnp.concatenate([zero, x[:, half:]], axis=1)], axis=0)


def _mm_pairs(a_list, b_list):
    return [_dot(a.astype(BF16), _block_diag(b.astype(BF16))) for a, b in zip(a_list, b_list)]


def _unit_lower_inverse(ms, eye, blockdiag, out):
    d = [jnp.where(blockdiag, m, 0.0) for m in ms]
    lo = [jnp.where(blockdiag, 0.0, m) for m in ms]
    d2 = _mm_pairs(d, d)
    yield
    d4, p = _square_and_times(d2, [eye - x for x in d], eye)
    yield
    d8, p = _square_and_times(d4, p, eye)
    yield
    p = _mm_pairs(p, [eye + x for x in d8])
    yield
    n = _mm_pairs(p, lo)
    yield
    n2 = _mm_pairs(n, n)
    yield
    qn = _mm_pairs([eye - x for x in n], [eye + x for x in n2])
    yield
    out[:] = _mm_pairs(qn, p)
    yield


def _square_and_times(xs, ps, eye):
    xb = [x.astype(BF16) for x in xs]
    both = _mm_pairs([jnp.concatenate([x, p.astype(BF16)], axis=0) for x, p in zip(xb, ps)],
                     [eye + x for x in xs])
    return ([r[:CHUNK] - x.astype(F32) for r, x in zip(both, xb)], [r[CHUNK:] for r in both])


def _interleave(major, minor, n_major, n_minor):
    done = 0
    for i, _ in enumerate(major):
        want = ((i + 1) * n_minor) // n_major
        while done < want and next(minor, StopIteration) is not StopIteration:
            done += 1
    for _ in minor:
        pass


PHASE1_STEPS = 11
PHASE2_STEPS_PER_CHUNK = 2


def _gdn_kernel(act_ref, nw_ref, s0_ref, o_ref, sfin_ref, state_ref,
                *, n_chunks):
    t = pl.program_id(1)

    @pl.when(t == 0)
    def _():
        state_ref[...] = s0_ref[...]

    dh = GDN_HEAD_DIM
    ri = lax.broadcasted_iota(jnp.int32, (CHUNK, 2 * CHUNK), 0)
    li = lax.broadcasted_iota(jnp.int32, (CHUNK, 2 * CHUNK), 1)
    left = li < CHUNK
    ci = jnp.where(left, li, li - CHUNK)
    causal = ri >= ci
    strict = ri > ci
    blockdiag = (ri // INV_BLOCK) == (ci // INV_BLOCK)
    eye = jnp.where(ri == ci, 1.0, 0.0).astype(F32)
    ri1 = lax.broadcasted_iota(jnp.int32, (CHUNK, CHUNK), 0)
    ci1 = lax.broadcasted_iota(jnp.int32, (CHUNK, CHUNK), 1)
    tril_b = jnp.where(ri1 >= ci1, 1.0, 0.0).astype(BF16)
    nw = nw_ref[...]

    def per_head(col_a, col_b):
        return jnp.concatenate([jnp.broadcast_to(col_a, (CHUNK, dh)),
                                jnp.broadcast_to(col_b, (CHUNK, dh))], axis=1)

    pairs = range(GDN_HEADS // 2)
    rows = [slice(c * CHUNK, (c + 1) * CHUNK) for c in range(n_chunks)]

    def col(x, lane):
        return x[:, lane:lane + 1]

    def phase1(chunks, res):
        items = [(c, j) for c in chunks for j in pairs]
        scc, gc, gct2 = {}, {}, {}
        for c in chunks:
            scc[c] = act_ref[0, rows[c], SC_COL:]
            hi = scc[c].astype(BF16)
            r1 = scc[c] - hi.astype(F32)
            mid = r1.astype(BF16)
            low = (r1 - mid.astype(F32)).astype(BF16)
            gc[c] = _dot(tril_b, hi) + _dot(tril_b, mid) + _dot(tril_b, low)
            gct2[c] = jnp.concatenate([gc[c], gc[c]], axis=0).T
        yield
        pair = lambda base, c, j: act_ref[0, rows[c], base + 2 * j * dh:base + (2 * j + 2) * dh]
        q = [pair(Q_COL, c, j) for c, j in items]
        k = [pair(K_COL, c, j) for c, j in items]
        v = [pair(V_COL, c, j) for c, j in items]
        g_a = [col(gc[c], GDN_HEADS + 2 * j) for c, j in items]
        g_b = [col(gc[c], GDN_HEADS + 2 * j + 1) for c, j in items]
        gl_a = [x[CHUNK - 1:CHUNK, :] for x in g_a]
        gl_b = [x[CHUNK - 1:CHUNK, :] for x in g_b]
        beta = [per_head(col(scc[c], 2 * j), col(scc[c], 2 * j + 1)) for c, j in items]
        eg = [per_head(jnp.exp(a), jnp.exp(b)) for a, b in zip(g_a, g_b)]
        kd_scale = [per_head(jnp.exp(la - a), jnp.exp(lb - b))
                    for a, b, la, lb in zip(g_a, g_b, gl_a, gl_b)]
        gcol = [jnp.where(left, a, b) for a, b in zip(g_a, g_b)]
        grow = [jnp.where(left[:1], gct2[c][GDN_HEADS + 2 * j:GDN_HEADS + 2 * j + 1, :],
                          gct2[c][GDN_HEADS + 2 * j + 1:GDN_HEADS + 2 * j + 2, :])
                for c, j in items]
        decay = [jnp.exp(jnp.where(causal, gc_ - gr_, MASK_NEG)) for gc_, gr_ in zip(gcol, grow)]
        kb = [k_ * b_ for k_, b_ in zip(k, beta)]
        kk_qk = [_dot_nt(jnp.concatenate([kb_, q_], axis=0).astype(BF16),
                         _block_diag(k_.astype(BF16))) for kb_, q_, k_ in zip(kb, q, k)]
        yield
        m = [jnp.where(strict, x[:CHUNK] * d_, 0.0) for x, d_ in zip(kk_qk, decay)]
        qk = [x[CHUNK:] * d_ for x, d_ in zip(kk_qk, decay)]
        tinv = []
        yield from _unit_lower_inverse(m, eye, blockdiag, tinv)
        vb = [v_ * b_ for v_, b_ in zip(v, beta)]
        kbg = [kb_ * e_ for kb_, e_ in zip(kb, eg)]
        uw = [_dot(_block_diag(t_.astype(BF16)),
                   jnp.concatenate([jnp.concatenate([vb_[:, :dh], kbg_[:, :dh]], axis=1),
                                    jnp.concatenate([vb_[:, dh:], kbg_[:, dh:]], axis=1)],
                                   axis=0).astype(BF16))
              for t_, vb_, kbg_ in zip(tinv, vb, kbg)]
        for i, item in enumerate(items):
            kd = k[i] * kd_scale[i]
            kdt = jnp.concatenate([kd[:, :dh], kd[:, dh:]], axis=0).T
            res[item] = dict(
                u=(uw[i][:CHUNK, :dh], uw[i][CHUNK:, :dh]),
                lhs1=jnp.concatenate([jnp.concatenate([uw[i][:CHUNK, dh:], uw[i][CHUNK:, dh:]],
                                                      axis=1), q[i] * eg[i]], axis=0).astype(BF16),
                lhs2=jnp.concatenate([qk[i], kdt], axis=0).astype(BF16),
                sdec=(jnp.exp(gl_a[i]), jnp.exp(gl_b[i])))
        yield

    def phase2(chunks, res):
        for c in chunks:
            r = [res[c, j] for j in pairs]
            s = [(state_ref[2 * j], state_ref[2 * j + 1]) for j in pairs]
            ws_qs = [_dot(x["lhs1"], _block_diag(jnp.concatenate(s_, axis=1).astype(BF16)))
                     for x, s_ in zip(r, s)]
            yield
            v_new = [jnp.concatenate([x["u"][0] - y[:CHUNK, :dh], x["u"][1] - y[:CHUNK, dh:]],
                                     axis=1) for x, y in zip(r, ws_qs)]
            r2 = [_dot(x["lhs2"], _block_diag(vn.astype(BF16))) for x, vn in zip(r, v_new)]
            for j in pairs:
                state_ref[2 * j] = s[j][0] * r[j]["sdec"][0] + r2[j][CHUNK:, :dh]
                state_ref[2 * j + 1] = s[j][1] * r[j]["sdec"][1] + r2[j][CHUNK:, dh:]
            yield
            for j in pairs:
                o2 = ws_qs[j][CHUNK:] + r2[j][:CHUNK]
                for half in range(2):
                    o = o2[:, half * dh:(half + 1) * dh]
                    o = o * lax.rsqrt(jnp.mean(o * o, axis=-1, keepdims=True) + RMS_EPS) * nw
                    cs = slice((2 * j + half) * dh, (2 * j + half + 1) * dh)
                    zh = act_ref[0, rows[c], Z_COL + cs.start:Z_COL + cs.stop]
                    o_ref[0, rows[c], cs] = (o * (zh * _sigmoid(zh))).astype(o_ref.dtype)

    groups = [list(range(g, min(g + GDN_GROUP, n_chunks))) for g in range(0, n_chunks, GDN_GROUP)]
    res = {}
    for _ in phase1(groups[0], res):
        pass
    for prev, cur in zip(groups[:-1], groups[1:]):
        _interleave(phase1(cur, res), phase2(prev, res), PHASE1_STEPS,
                    PHASE2_STEPS_PER_CHUNK * len(prev))
    for _ in phase2(groups[-1], res):
        pass

    @pl.when(t == pl.num_programs(1) - 1)
    def _():
        sfin_ref[0] = state_ref[...]


def _gdn(act, norm_w, state0, tm):
    bsz, t_len, _ = act.shape
    state_shape = (GDN_HEADS, GDN_HEAD_DIM, GDN_HEAD_DIM)
    return pl.pallas_call(
        functools.partial(_gdn_kernel, n_chunks=tm // CHUNK),
        grid=(bsz, t_len // tm),
        in_specs=[_row_spec(tm, ACT_WIDTH), _const_spec(norm_w.shape), _const_spec(state0.shape)],
        out_specs=(_row_spec(tm, GDN_WIDTH),
                   pl.BlockSpec((1,) + state_shape, lambda b, t: (b, 0, 0, 0))),
        out_shape=(jax.ShapeDtypeStruct((bsz, t_len, GDN_WIDTH), BF16),
                   jax.ShapeDtypeStruct((bsz,) + state_shape, F32)),
        scratch_shapes=[pltpu.VMEM(state_shape, F32)],
        compiler_params=_params(),
        name="gdn_scan",
    )(act, norm_w, state0)


def _sconv_kernel(h_ref, win_ref, cw_ref, wout_ref, g_ref, b_ref, halo_ref, o_ref, tail_ref,
                  carry_ref, act_ref, *, alpha, tm):
    t = pl.program_id(1)

    @pl.when(t == 0)
    def _():
        carry_ref[...] = halo_ref[...]

    width = cw_ref.shape[1]
    n_chunks = width // SCONV_COL_CHUNK
    rs = min(SUB_ROWS, tm)

    for r0 in range(0, tm, rs):
        rows = slice(r0, r0 + rs)
        hv = h_ref[0, rows, :]
        hb = hv.astype(BF16)

        def proj(c):
            lo = c * SCONV_COL_CHUNK
            return [_dot(hb, win_ref[:, part * width + lo:part * width + lo + SCONV_COL_CHUNK])
                    for part in range(3)]

        nxt = proj(0)
        for c in range(n_chunks):
            cs = slice(c * SCONV_COL_CHUNK, (c + 1) * SCONV_COL_CHUNK)
            b_gate, c_gate, xv = nxt
            if c + 1 < n_chunks:
                nxt = proj(c + 1)
            cx = c_gate * xv
            u = _causal_taps(carry_ref[:, cs], cx, _spread_taps(cw_ref[:, cs]))
            carry_ref[:, cs] = cx[rs - HALO:rs, :]
            act_ref[rows, cs] = (b_gate * u).astype(BF16)
        y = _dot(act_ref[rows, :], wout_ref[...])
        o_ref[0, rows, :] = _layer_norm(alpha * hv + y, g_ref[...], b_ref[...])

    @pl.when(t == pl.num_programs(1) - 1)
    def _():
        tail_ref[0] = carry_ref[...]


def _sconv(h, w_in, w_out, layer, conv_w, g, b, halo, alpha, tm):
    bsz, t_len, d = h.shape
    width = conv_w.shape[1]
    return pl.pallas_call(
        functools.partial(_sconv_kernel, alpha=alpha, tm=tm),
        grid=(bsz, t_len // tm),
        in_specs=[_row_spec(tm, d), _layer_spec(w_in, layer), _const_spec(conv_w.shape),
                  _layer_spec(w_out, layer), _const_spec(g.shape), _const_spec(b.shape),
                  _const_spec(halo.shape)],
        out_specs=(_row_spec(tm, d), _tail_spec(width)),
        out_shape=(jax.ShapeDtypeStruct((bsz, t_len, d), F32),
                   jax.ShapeDtypeStruct((bsz, HALO, width), F32)),
        scratch_shapes=[pltpu.VMEM((HALO, width), F32), pltpu.VMEM((tm, width), BF16)],
        compiler_params=_params(),
        name="sconv_mixer",
    )(h, w_in, conv_w, w_out, g, b, halo)


def _ffn_kernel(*refs, alpha, tm, mixer_tail):
    if mixer_tail:
        a_ref, wo_ref, g1_ref, b1_ref = refs[:4]
        refs = refs[4:]
    (h_ref, wup_ref, cw_ref, wd_ref, g_ref, b_ref, halo_ref, o_ref, tail_ref,
     carry_ref, act_ref) = refs
    t = pl.program_id(1)

    @pl.when(t == 0)
    def _():
        carry_ref[...] = halo_ref[...]

    d_ff = cw_ref.shape[1]
    n_chunks = d_ff // FFN_COL_CHUNK
    col_slice = lambda c: slice(c * FFN_COL_CHUNK, (c + 1) * FFN_COL_CHUNK)
    rs = min(SUB_ROWS, tm)

    subs = [slice(r0, r0 + rs) for r0 in range(0, tm, rs)]
    if mixer_tail:
        proj = [_dot(a_ref[0, rows, :], wo_ref[...]) for rows in subs]
        for rows, y in zip(subs, proj):
            o_ref[0, rows, :] = _layer_norm(alpha * h_ref[0, rows, :] + y, g1_ref[...], b1_ref[...])
    x_ref = o_ref if mixer_tail else h_ref
    for rows in subs:
        hv = x_ref[0, rows, :]
        hb = hv.astype(BF16)
        up = lambda c: (_dot(hb, wup_ref[:, col_slice(c)]),
                        _dot(hb, wup_ref[:, d_ff + c * FFN_COL_CHUNK:d_ff + (c + 1) * FFN_COL_CHUNK]))
        nxt = up(0)
        y = None
        group_start = 0
        for c in range(n_chunks):
            cs = col_slice(c)
            u, gate = nxt
            if c + 1 < n_chunks:
                nxt = up(c + 1)
            uc = _causal_taps(carry_ref[:, cs], u, _spread_taps(cw_ref[:, cs]))
            carry_ref[:, cs] = u[rs - HALO:rs, :]
            act_ref[rows, cs] = (uc * _sigmoid(uc) * gate).astype(BF16)
            if (c + 1 - group_start) == FFN_DOWN_GROUP or c + 1 == n_chunks:
                ks = slice(group_start * FFN_COL_CHUNK, (c + 1) * FFN_COL_CHUNK)
                part = _dot(act_ref[rows, ks], wd_ref[ks, :])
                y = part if y is None else y + part
                group_start = c + 1
        o_ref[0, rows, :] = _layer_norm(alpha * hv + y, g_ref[...], b_ref[...])

    @pl.when(t == pl.num_programs(1) - 1)
    def _():
        tail_ref[0] = carry_ref[...]


def _ffn(h, w_up, w_down, layer, conv_w, g, b, halo, alpha, tm, mixer=None):
    bsz, t_len, d = h.shape
    d_ff = conv_w.shape[1]
    operands = [h, w_up, conv_w, w_down, g, b, halo]
    in_specs = [_row_spec(tm, d), _layer_spec(w_up, layer), _const_spec(conv_w.shape),
                _layer_spec(w_down, layer)] + [_const_spec(x.shape) for x in (g, b, halo)]
    if mixer is not None:
        a, w_o, w_o_layer, g1, b1 = mixer
        operands = [a, w_o, g1, b1] + operands
        in_specs = [_row_spec(tm, a.shape[2]), _layer_spec(w_o, w_o_layer),
                    _const_spec(g1.shape), _const_spec(b1.shape)] + in_specs
    return pl.pallas_call(
        functools.partial(_ffn_kernel, alpha=alpha, tm=tm, mixer_tail=mixer is not None),
        grid=(bsz, t_len // tm),
        in_specs=in_specs,
        out_specs=(_row_spec(tm, d), _tail_spec(d_ff)),
        out_shape=(jax.ShapeDtypeStruct((bsz, t_len, d), F32),
                   jax.ShapeDtypeStruct((bsz, HALO, d_ff), F32)),
        scratch_shapes=[pltpu.VMEM((HALO, d_ff), F32), pltpu.VMEM((tm, d_ff), BF16)],
        compiler_params=_params(),
        name="conv_ffn",
    )(*operands)


def _gdn_mixer(hm, hx, w_in_f32, w_in, layer, conv_w, a_log, dt_bias, norm_w):
    w3 = 3 * GDN_WIDTH
    wba = jnp.pad(w_in_f32[:, w3 + GDN_WIDTH:], ((0, 0), (0, LANES - 2 * GDN_HEADS))).astype(BF16)
    lane_pad = (GDN_HEADS, LANES - 2 * GDN_HEADS)
    gscale = jnp.pad(-jnp.exp(a_log.astype(F32)), lane_pad)[None]
    dtb = jnp.pad(dt_bias.astype(F32), lane_pad)[None]
    nw = norm_w.astype(F32)[None]
    state_shape = (GDN_HEADS, GDN_HEAD_DIM, GDN_HEAD_DIM)

    act_m, tail_m = _gdn_in(hm, w_in, layer, wba, conv_w, gscale, dtb,
                            jnp.zeros((HALO, w3), F32), N_META)
    front = ((0, 0), (CHUNK - N_META, 0), (0, 0))
    om, state_m = _gdn(jnp.pad(act_m, front), nw, jnp.zeros(state_shape, F32), CHUNK)

    act_x, _ = _gdn_in(hx, w_in, layer, wba, conv_w, gscale, dtb, tail_m[0], ROW_TILE)
    ox, _ = _gdn(act_x, nw, state_m[0], GDN_ROW_TILE)
    return om[:, CHUNK - N_META:], ox


def _sconv_layer(hm, hx, w_in, w_out, layer, conv_w, ln_g, ln_b, alpha):
    g2, b2 = ln_g[None], ln_b[None]
    width = conv_w.shape[1]
    hm_new, tail_m = _sconv(hm, w_in, w_out, layer, conv_w, g2, b2,
                            jnp.zeros((HALO, width), F32), alpha, N_META)
    hx_new, _ = _sconv(hx, w_in, w_out, layer, conv_w, g2, b2, tail_m[0], alpha, ROW_TILE)
    return hm_new, hx_new


def _ffn_layer(hm, hx, w_up, w_down, layer, conv_w, ln_g, ln_b, alpha, mixer=None):
    d_ff = conv_w.shape[1]
    g2, b2 = ln_g[None], ln_b[None]
    mix_m = mix_x = None
    if mixer is not None:
        am, ax, w_o, w_o_layer, g1, b1 = mixer
        tail = (w_o, w_o_layer, g1[None], b1[None])
        mix_m, mix_x = (am,) + tail, (ax,) + tail
    hm_new, tail_m = _ffn(hm, w_up, w_down, layer, conv_w, g2, b2, jnp.zeros((HALO, d_ff), F32),
                          alpha, N_META, mix_m)
    hx_new, _ = _ffn(hx, w_up, w_down, layer, conv_w, g2, b2, tail_m[0], alpha, ROW_TILE, mix_x)
    return hm_new, hx_new


def kernel(x, meta, a_w_in, a_conv, a_log, a_dt_bias, a_norm, a_w_out, b_w_in, b_conv, b_w_out,
           ln_mix_g, ln_mix_b, ffn_w_up, ffn_conv, ffn_w_down, ln_ffn_g, ln_ffn_b):
    depth = ln_mix_g.shape[0]
    alpha = (2.0 * depth) ** 0.25
    assert x.shape[1] % ROW_TILE == 0 and meta.shape[0] == N_META
    hx = x
    hm = meta.astype(x.dtype)[None]
    a_in, a_out = a_w_in.astype(BF16), a_w_out.astype(BF16)
    b_in, b_out = b_w_in.astype(BF16), b_w_out.astype(BF16)
    f_up, f_down = ffn_w_up.astype(BF16), ffn_w_down.astype(BF16)
    for i in range(depth):
        j = i // 2
        mixer = None
        if i % 2 == 0:
            am, ax = _gdn_mixer(hm, hx, a_w_in[j], a_in, j, a_conv[j], a_log[j], a_dt_bias[j],
                                a_norm[j])
            mixer = (am, ax, a_out, j, ln_mix_g[i], ln_mix_b[i])
        else:
            hm, hx = _sconv_layer(hm, hx, b_in, b_out, j, b_conv[j], ln_mix_g[i], ln_mix_b[i],
                                  alpha)
        hm, hx = _ffn_layer(hm, hx, f_up, f_down, i, ffn_conv[i], ln_ffn_g[i], ln_ffn_b[i], alpha,
                            mixer)
    return hx
```

```python
import functools

import jax
import jax.numpy as jnp
from jax import lax
from jax.experimental import pallas as pl
from jax.experimental.pallas import tpu as pltpu

F32 = jnp.float32
BF16 = jnp.bfloat16

N_META = 16
GDN_HEADS = 8
GDN_HEAD_DIM = 128
GDN_WIDTH = GDN_HEADS * GDN_HEAD_DIM
Q_SCALE = GDN_HEAD_DIM ** -0.5
CHUNK = 64
INV_BLOCK = 16
HALO = 8
LANES = 128
Q_COL, K_COL, V_COL, Z_COL, SC_COL = (i * GDN_WIDTH for i in range(5))
ACT_WIDTH = SC_COL + LANES
LN_EPS = 1e-5
RMS_EPS = 1e-6
L2_EPS = 1e-6
MASK_NEG = -1e30

ROW_TILE = 512
SUB_ROWS = 256
GDN_ROW_TILE = 512
GDN_GROUP = 4
QKV_COL_CHUNK = 512
EW_ROWS = 64
MXU_AHEAD = 2
QKV_SLOTS = MXU_AHEAD + 1
FFN_COL_CHUNK = 256
FFN_DOWN_GROUP = 4
SCONV_COL_CHUNK = 256
VMEM_LIMIT = 56 * 1024 * 1024


def _dot(a, b):
    return jnp.dot(a, b, preferred_element_type=F32)


def _dot_nt(a, b):
    return lax.dot_general(a, b, (((1,), (1,)), ((), ())), preferred_element_type=F32)


def _sigmoid(x):
    return 1.0 / (1.0 + jnp.exp(-x))


def _layer_norm(x, g, b):
    mu = jnp.mean(x, axis=-1, keepdims=True)
    xc = x - mu
    var = jnp.mean(xc * xc, axis=-1, keepdims=True)
    return xc * lax.rsqrt(var + LN_EPS) * g + b


def _const_spec(shape):
    nd = len(shape)
    return pl.BlockSpec(shape, lambda *_, _nd=nd: (0,) * _nd, pipeline_mode=pl.Buffered(1))


def _layer_spec(stacked, layer):
    nd = stacked.ndim - 1
    return pl.BlockSpec((None,) + stacked.shape[1:], lambda *_, _nd=nd: (layer,) + (0,) * _nd,
                        pipeline_mode=pl.Buffered(1))


def _row_spec(tm, width):
    return pl.BlockSpec((1, tm, width), lambda b, t: (b, t, 0))


def _tail_spec(width):
    return pl.BlockSpec((1, HALO, width), lambda b, t: (b, 0, 0))


def _params():
    return pltpu.CompilerParams(dimension_semantics=("arbitrary", "arbitrary"),
                                vmem_limit_bytes=VMEM_LIMIT)


def _spread_taps(taps):
    return [jnp.broadcast_to(taps[j:j + 1, :], (HALO, taps.shape[1]))[None]
            for j in range(taps.shape[0])]


def _causal_taps(carry, cur, taps):
    width = len(taps)
    tm, c = cur.shape
    assert width - 1 <= HALO
    tiles = jnp.concatenate([carry, cur], axis=0).reshape(tm // HALO + 1, HALO, c)
    first = lax.broadcasted_iota(jnp.int32, (tm // HALO, HALO, c), 1) == 0
    acc = tiles * taps[0]
    for j in range(1, width):
        rot = pltpu.roll(acc, 1, 1)
        shifted = jnp.where(first, rot[:-1], rot[1:])
        if j < width - 1:
            acc = tiles * taps[j] + jnp.concatenate([rot[:1], shifted], axis=0)
        else:
            acc = tiles[1:] * taps[j] + shifted
    return acc.reshape(tm, c)


def _gdn_in_kernel(h_ref, w_ref, wba_ref, cw_ref, gs_ref, dtb_ref, halo_ref,
                   act_ref, tail_ref, st_ref, carry_ref, *, tm):
    t = pl.program_id(1)

    @pl.when(t == 0)
    def _():
        carry_ref[...] = halo_ref[...]

    hb = h_ref[0].astype(BF16)
    out_col = (Q_COL, K_COL, V_COL)
    heads_per_chunk = QKV_COL_CHUNK // GDN_HEAD_DIM
    chunks_per_out = GDN_WIDTH // QKV_COL_CHUNK
    n_chunks = 3 * chunks_per_out
    col_slice = lambda c: slice(c * QKV_COL_CHUNK, (c + 1) * QKV_COL_CHUNK)
    rb = min(EW_ROWS, tm)

    def project(c):
        st_ref[c % QKV_SLOTS, HALO:HALO + tm, :] = _dot(hb, w_ref[:, col_slice(c)])

    def project_z(i):
        zs = slice(Z_COL + i * QKV_COL_CHUNK, Z_COL + (i + 1) * QKV_COL_CHUNK)
        ws = slice(3 * GDN_WIDTH + i * QKV_COL_CHUNK, 3 * GDN_WIDTH + (i + 1) * QKV_COL_CHUNK)
        act_ref[0, :, zs] = _dot(hb, w_ref[:, ws])

    for c in range(min(MXU_AHEAD, n_chunks)):
        project(c)
    ba = None
    for c in range(n_chunks):
        cs = col_slice(c)
        st = st_ref.at[c % QKV_SLOTS]
        if c + MXU_AHEAD < n_chunks:
            project(c + MXU_AHEAD)
        if c < chunks_per_out:
            project_z(c)
        elif ba is None:
            ba = _dot(hb, wba_ref[...])
        st[0:HALO, :] = carry_ref[:, cs]
        carry_ref[:, cs] = st[tm:tm + HALO, :]
        which = c // chunks_per_out
        for hh in range(heads_per_chunk):
            hs = slice(hh * GDN_HEAD_DIM, (hh + 1) * GDN_HEAD_DIM)
            col = (c % chunks_per_out) * QKV_COL_CHUNK + hh * GDN_HEAD_DIM
            taps = _spread_taps(cw_ref[:, c * QKV_COL_CHUNK + hh * GDN_HEAD_DIM:
                                       c * QKV_COL_CHUNK + (hh + 1) * GDN_HEAD_DIM])
            for r0 in range(0, tm, rb):
                blk = st[r0:r0 + rb + HALO, hs]
                y = _causal_taps(blk[:HALO], blk[HALO:], taps)
                y = y * _sigmoid(y)
                if which < 2:
                    inv = lax.rsqrt(jnp.sum(y * y, axis=-1, keepdims=True) + L2_EPS)
                    y = y * (inv * Q_SCALE if which == 0 else inv)
                act_ref[0, r0:r0 + rb, out_col[which] + col:out_col[which] + col + GDN_HEAD_DIM] = y

    lane = lax.broadcasted_iota(jnp.int32, ba.shape, 1)
    xg = ba + dtb_ref[...]
    softplus = jnp.maximum(xg, 0.0) + jnp.log(1.0 + jnp.exp(-jnp.abs(xg)))
    act_ref[0, :, SC_COL:] = jnp.where(lane < GDN_HEADS, _sigmoid(ba), gs_ref[...] * softplus)

    @pl.when(t == pl.num_programs(1) - 1)
    def _():
        tail_ref[0] = carry_ref[...]


def _gdn_in(h, w_in, layer, wba, conv_w, gscale, dtb, halo, tm):
    bsz, t_len, d = h.shape
    w3 = 3 * GDN_WIDTH
    row = lambda width: _row_spec(tm, width)
    out_shape = (
        jax.ShapeDtypeStruct((bsz, t_len, ACT_WIDTH), F32),
        jax.ShapeDtypeStruct((bsz, HALO, w3), F32),
    )
    return pl.pallas_call(
        functools.partial(_gdn_in_kernel, tm=tm),
        grid=(bsz, t_len // tm),
        in_specs=[row(d), _layer_spec(w_in, layer), _const_spec(wba.shape),
                  _const_spec(conv_w.shape), _const_spec(gscale.shape), _const_spec(dtb.shape),
                  _const_spec(halo.shape)],
        out_specs=(row(ACT_WIDTH), _tail_spec(w3)),
        out_shape=out_shape,
        scratch_shapes=[pltpu.VMEM((QKV_SLOTS, tm + HALO, QKV_COL_CHUNK), F32),
                        pltpu.VMEM((HALO, w3), F32)],
        compiler_params=_params(),
        name="gdn_in",
    )(h, w_in, wba, conv_w, gscale, dtb, halo)


def _block_diag(x):
    half = x.shape[1] // 2
    zero = jnp.zeros((x.shape[0], half), x.dtype)
    return jnp.concatenate([jnp.concatenate([x[:, :half], zero], axis=1),
                            jnp.concatenate([zero, x[:, half:]], axis=1)], axis=0)


def _mm_pairs(a_list, b_list):
    return [_dot(a.astype(BF16), _block_diag(b.astype(BF16))) for a, b in zip(a_list, b_list)]


def _unit_lower_inverse(ms, eye, blockdiag, out):
    d = [jnp.where(blockdiag, m, 0.0) for m in ms]
    lo = [jnp.where(blockdiag, 0.0, m) for m in ms]
    d2 = _mm_pairs(d, d)
    yield
    d4, p = _square_and_times(d2, [eye - x for x in d], eye)
    yield
    d8, p = _square_and_times(d4, p, eye)
    yield
    p = _mm_pairs(p, [eye + x for x in d8])
    yield
    n = _mm_pairs(p, lo)
    yield
    n2 = _mm_pairs(n, n)
    yield
    qn = _mm_pairs([eye - x for x in n], [eye + x for x in n2])
    yield
    out[:] = _mm_pairs(qn, p)
    yield


def _square_and_times(xs, ps, eye):
    xb = [x.astype(BF16) for x in xs]
    both = _mm_pairs([jnp.concatenate([x, p.astype(BF16)], axis=0) for x, p in zip(xb, ps)],
                     [eye + x for x in xs])
    return ([r[:CHUNK] - x.astype(F32) for r, x in zip(both, xb)], [r[CHUNK:] for r in both])


def _interleave(major, minor, n_major, n_minor):
    done = 0
    for i, _ in enumerate(major):
        want = ((i + 1) * n_minor) // n_major
        while done < want and next(minor, StopIteration) is not StopIteration:
            done += 1
    for _ in minor:
        pass


PHASE1_STEPS = 11
PHASE2_STEPS_PER_CHUNK = 2


def _gdn_kernel(act_ref, nw_ref, s0_ref, o_ref, sfin_ref, state_ref,
                *, n_chunks):
    t = pl.program_id(1)

    @pl.when(t == 0)
    def _():
        state_ref[...] = s0_ref[...]

    dh = GDN_HEAD_DIM
    ri = lax.broadcasted_iota(jnp.int32, (CHUNK, 2 * CHUNK), 0)
    li = lax.broadcasted_iota(jnp.int32, (CHUNK, 2 * CHUNK), 1)
    left = li < CHUNK
    ci = jnp.where(left, li, li - CHUNK)
    causal = ri >= ci
    strict = ri > ci
    blockdiag = (ri // INV_BLOCK) == (ci // INV_BLOCK)
    eye = jnp.where(ri == ci, 1.0, 0.0).astype(F32)
    ri1 = lax.broadcasted_iota(jnp.int32, (CHUNK, CHUNK), 0)
    ci1 = lax.broadcasted_iota(jnp.int32, (CHUNK, CHUNK), 1)
    tril_b = jnp.where(ri1 >= ci1, 1.0, 0.0).astype(BF16)
    nw = nw_ref[...]

    def per_head(col_a, col_b):
        return jnp.concatenate([jnp.broadcast_to(col_a, (CHUNK, dh)),
                                jnp.broadcast_to(col_b, (CHUNK, dh))], axis=1)

    pairs = range(GDN_HEADS // 2)
    rows = [slice(c * CHUNK, (c + 1) * CHUNK) for c in range(n_chunks)]

    def col(x, lane):
        return x[:, lane:lane + 1]

    def phase1(chunks, res):
        items = [(c, j) for c in chunks for j in pairs]
        scc, gc, gct2 = {}, {}, {}
        for c in chunks:
            scc[c] = act_ref[0, rows[c], SC_COL:]
            hi = scc[c].astype(BF16)
            r1 = scc[c] - hi.astype(F32)
            mid = r1.astype(BF16)
            low = (r1 - mid.astype(F32)).astype(BF16)
            gc[c] = _dot(tril_b, hi) + _dot(tril_b, mid) + _dot(tril_b, low)
            gct2[c] = jnp.concatenate([gc[c], gc[c]], axis=0).T
        yield
        pair = lambda base, c, j: act_ref[0, rows[c], base + 2 * j * dh:base + (2 * j + 2) * dh]
        q = [pair(Q_COL, c, j) for c, j in items]
        k = [pair(K_COL, c, j) for c, j in items]
        v = [pair(V_COL, c, j) for c, j in items]
        g_a = [col(gc[c], GDN_HEADS + 2 * j) for c, j in items]
        g_b = [col(gc[c], GDN_HEADS + 2 * j + 1) for c, j in items]
        gl_a = [x[CHUNK - 1:CHUNK, :] for x in g_a]
        gl_b = [x[CHUNK - 1:CHUNK, :] for x in g_b]
        beta = [per_head(col(scc[c], 2 * j), col(scc[c], 2 * j + 1)) for c, j in items]
        eg = [per_head(jnp.exp(a), jnp.exp(b)) for a, b in zip(g_a, g_b)]
        kd_scale = [per_head(jnp.exp(la - a), jnp.exp(lb - b))
                    for a, b, la, lb in zip(g_a, g_b, gl_a, gl_b)]
        gcol = [jnp.where(left, a, b) for a, b in zip(g_a, g_b)]
        grow = [jnp.where(left[:1], gct2[c][GDN_HEADS + 2 * j:GDN_HEADS + 2 * j + 1, :],
                          gct2[c][GDN_HEADS + 2 * j + 1:GDN_HEADS + 2 * j + 2, :])
                for c, j in items]
        decay = [jnp.exp(jnp.where(causal, gc_ - gr_, MASK_NEG)) for gc_, gr_ in zip(gcol, grow)]
        kb = [k_ * b_ for k_, b_ in zip(k, beta)]
        kk_qk = [_dot_nt(jnp.concatenate([kb_, q_], axis=0).astype(BF16),
                         _block_diag(k_.astype(BF16))) for kb_, q_, k_ in zip(kb, q, k)]
        yield
        m = [jnp.where(strict, x[:CHUNK] * d_, 0.0) for x, d_ in zip(kk_qk, decay)]
        qk = [x[CHUNK:] * d_ for x, d_ in zip(kk_qk, decay)]
        tinv = []
        yield from _unit_lower_inverse(m, eye, blockdiag, tinv)
        vb = [v_ * b_ for v_, b_ in zip(v, beta)]
        kbg = [kb_ * e_ for kb_, e_ in zip(kb, eg)]
        uw = [_dot(_block_diag(t_.astype(BF16)),
                   jnp.concatenate([jnp.concatenate([vb_[:, :dh], kbg_[:, :dh]], axis=1),
                                    jnp.concatenate([vb_[:, dh:], kbg_[:, dh:]], axis=1)],
                                   axis=0).astype(BF16))
              for t_, vb_, kbg_ in zip(tinv, vb, kbg)]
        for i, item in enumerate(items):
            kd = k[i] * kd_scale[i]
            kdt = jnp.concatenate([kd[:, :dh], kd[:, dh:]], axis=0).T
            res[item] = dict(
                u=(uw[i][:CHUNK, :dh], uw[i][CHUNK:, :dh]),
                lhs1=jnp.concatenate([jnp.concatenate([uw[i][:CHUNK, dh:], uw[i][CHUNK:, dh:]],
                                                      axis=1), q[i] * eg[i]], axis=0).astype(BF16),
                lhs2=jnp.concatenate([qk[i], kdt], axis=0).astype(BF16),
                sdec=(jnp.exp(gl_a[i]), jnp.exp(gl_b[i])))
        yield

    def phase2(chunks, res):
        for c in chunks:
            r = [res[c, j] for j in pairs]
            s = [(state_ref[2 * j], state_ref[2 * j + 1]) for j in pairs]
            ws_qs = [_dot(x["lhs1"], _block_diag(jnp.concatenate(s_, axis=1).astype(BF16)))
                     for x, s_ in zip(r, s)]
            yield
            v_new = [jnp.concatenate([x["u"][0] - y[:CHUNK, :dh], x["u"][1] - y[:CHUNK, dh:]],
                                     axis=1) for x, y in zip(r, ws_qs)]
            r2 = [_dot(x["lhs2"], _block_diag(vn.astype(BF16))) for x, vn in zip(r, v_new)]
            for j in pairs:
                state_ref[2 * j] = s[j][0] * r[j]["sdec"][0] + r2[j][CHUNK:, :dh]
                state_ref[2 * j + 1] = s[j][1] * r[j]["sdec"][1] + r2[j][CHUNK:, dh:]
            yield
            for j in pairs:
                o2 = ws_qs[j][CHUNK:] + r2[j][:CHUNK]
                for half in range(2):
                    o = o2[:, half * dh:(half + 1) * dh]
                    o = o * lax.rsqrt(jnp.mean(o * o, axis=-1, keepdims=True) + RMS_EPS) * nw
                    cs = slice((2 * j + half) * dh, (2 * j + half + 1) * dh)
                    zh = act_ref[0, rows[c], Z_COL + cs.start:Z_COL + cs.stop]
                    o_ref[0, rows[c], cs] = (o * (zh * _sigmoid(zh))).astype(o_ref.dtype)

    groups = [list(range(g, min(g + GDN_GROUP, n_chunks))) for g in range(0, n_chunks, GDN_GROUP)]
    res = {}
    for _ in phase1(groups[0], res):
        pass
    for prev, cur in zip(groups[:-1], groups[1:]):
        _interleave(phase1(cur, res), phase2(prev, res), PHASE1_STEPS,
                    PHASE2_STEPS_PER_CHUNK * len(prev))
    for _ in phase2(groups[-1], res):
        pass

    @pl.when(t == pl.num_programs(1) - 1)
    def _():
        sfin_ref[0] = state_ref[...]


def _gdn(act, norm_w, state0, tm):
    bsz, t_len, _ = act.shape
    state_shape = (GDN_HEADS, GDN_HEAD_DIM, GDN_HEAD_DIM)
    return pl.pallas_call(
        functools.partial(_gdn_kernel, n_chunks=tm // CHUNK),
        grid=(bsz, t_len // tm),
        in_specs=[_row_spec(tm, ACT_WIDTH), _const_spec(norm_w.shape), _const_spec(state0.shape)],
        out_specs=(_row_spec(tm, GDN_WIDTH),
                   pl.BlockSpec((1,) + state_shape, lambda b, t: (b, 0, 0, 0))),
        out_shape=(jax.ShapeDtypeStruct((bsz, t_len, GDN_WIDTH), BF16),
                   jax.ShapeDtypeStruct((bsz,) + state_shape, F32)),
        scratch_shapes=[pltpu.VMEM(state_shape, F32)],
        compiler_params=_params(),
        name="gdn_scan",
    )(act, norm_w, state0)


def _sconv_kernel(h_ref, win_ref, cw_ref, wout_ref, g_ref, b_ref, halo_ref, o_ref, tail_ref,
                  carry_ref, act_ref, *, alpha, tm):
    t = pl.program_id(1)

    @pl.when(t == 0)
    def _():
        carry_ref[...] = halo_ref[...]

    width = cw_ref.shape[1]
    n_chunks = width // SCONV_COL_CHUNK
    rs = min(SUB_ROWS, tm)

    for r0 in range(0, tm, rs):
        rows = slice(r0, r0 + rs)
        hv = h_ref[0, rows, :]
        hb = hv.astype(BF16)

        def proj(c):
            lo = c * SCONV_COL_CHUNK
            return [_dot(hb, win_ref[:, part * width + lo:part * width + lo + SCONV_COL_CHUNK])
                    for part in range(3)]

        nxt = proj(0)
        for c in range(n_chunks):
            cs = slice(c * SCONV_COL_CHUNK, (c + 1) * SCONV_COL_CHUNK)
            b_gate, c_gate, xv = nxt
            if c + 1 < n_chunks:
                nxt = proj(c + 1)
            cx = c_gate * xv
            u = _causal_taps(carry_ref[:, cs], cx, _spread_taps(cw_ref[:, cs]))
            carry_ref[:, cs] = cx[rs - HALO:rs, :]
            act_ref[rows, cs] = (b_gate * u).astype(BF16)
        y = _dot(act_ref[rows, :], wout_ref[...])
        o_ref[0, rows, :] = _layer_norm(alpha * hv + y, g_ref[...], b_ref[...])

    @pl.when(t == pl.num_programs(1) - 1)
    def _():
        tail_ref[0] = carry_ref[...]


def _sconv(h, w_in, w_out, layer, conv_w, g, b, halo, alpha, tm):
    bsz, t_len, d = h.shape
    width = conv_w.shape[1]
    return pl.pallas_call(
        functools.partial(_sconv_kernel, alpha=alpha, tm=tm),
        grid=(bsz, t_len // tm),
        in_specs=[_row_spec(tm, d), _layer_spec(w_in, layer), _const_spec(conv_w.shape),
                  _layer_spec(w_out, layer), _const_spec(g.shape), _const_spec(b.shape),
                  _const_spec(halo.shape)],
        out_specs=(_row_spec(tm, d), _tail_spec(width)),
        out_shape=(jax.ShapeDtypeStruct((bsz, t_len, d), F32),
                   jax.ShapeDtypeStruct((bsz, HALO, width), F32)),
        scratch_shapes=[pltpu.VMEM((HALO, width), F32), pltpu.VMEM((tm, width), BF16)],
        compiler_params=_params(),
        name="sconv_mixer",
    )(h, w_in, conv_w, w_out, g, b, halo)


def _ffn_kernel(*refs, alpha, tm, mixer_tail):
    if mixer_tail:
        a_ref, wo_ref, g1_ref, b1_ref = refs[:4]
        refs = refs[4:]
    (h_ref, wup_ref, cw_ref, wd_ref, g_ref, b_ref, halo_ref, o_ref, tail_ref,
     carry_ref, act_ref) = refs
    t = pl.program_id(1)

    @pl.when(t == 0)
    def _():
        carry_ref[...] = halo_ref[...]

    d_ff = cw_ref.shape[1]
    n_chunks = d_ff // FFN_COL_CHUNK
    col_slice = lambda c: slice(c * FFN_COL_CHUNK, (c + 1) * FFN_COL_CHUNK)
    rs = min(SUB_ROWS, tm)

    subs = [slice(r0, r0 + rs) for r0 in range(0, tm, rs)]
    if mixer_tail:
        proj = [_dot(a_ref[0, rows, :], wo_ref[...]) for rows in subs]
        for rows, y in zip(subs, proj):
            o_ref[0, rows, :] = _layer_norm(alpha * h_ref[0, rows, :] + y, g1_ref[...], b1_ref[...])
    x_ref = o_ref if mixer_tail else h_ref
    for rows in subs:
        hv = x_ref[0, rows, :]
        hb = hv.astype(BF16)
        up = lambda c: (_dot(hb, wup_ref[:, col_slice(c)]),
                        _dot(hb, wup_ref[:, d_ff + c * FFN_COL_CHUNK:d_ff + (c + 1) * FFN_COL_CHUNK]))
        nxt = up(0)
        y = None
        group_start = 0
        for c in range(n_chunks):
            cs = col_slice(c)
            u, gate = nxt
            if c + 1 < n_chunks:
                nxt = up(c + 1)
            uc = _causal_taps(carry_ref[:, cs], u, _spread_taps(cw_ref[:, cs]))
            carry_ref[:, cs] = u[rs - HALO:rs, :]
            act_ref[rows, cs] = (uc * _sigmoid(uc) * gate).astype(BF16)
            if (c + 1 - group_start) == FFN_DOWN_GROUP or c + 1 == n_chunks:
                ks = slice(group_start * FFN_COL_CHUNK, (c + 1) * FFN_COL_CHUNK)
                part = _dot(act_ref[rows, ks], wd_ref[ks, :])
                y = part if y is None else y + part
                group_start = c + 1
        o_ref[0, rows, :] = _layer_norm(alpha * hv + y, g_ref[...], b_ref[...])

    @pl.when(t == pl.num_programs(1) - 1)
    def _():
        tail_ref[0] = carry_ref[...]


def _ffn(h, w_up, w_down, layer, conv_w, g, b, halo, alpha, tm, mixer=None):
    bsz, t_len, d = h.shape
    d_ff = conv_w.shape[1]
    operands = [h, w_up, conv_w, w_down, g, b, halo]
    in_specs = [_row_spec(tm, d), _layer_spec(w_up, layer), _const_spec(conv_w.shape),
                _layer_spec(w_down, layer)] + [_const_spec(x.shape) for x in (g, b, halo)]
    if mixer is not None:
        a, w_o, w_o_layer, g1, b1 = mixer
        operands = [a, w_o, g1, b1] + operands
        in_specs = [_row_spec(tm, a.shape[2]), _layer_spec(w_o, w_o_layer),
                    _const_spec(g1.shape), _const_spec(b1.shape)] + in_specs
    return pl.pallas_call(
        functools.partial(_ffn_kernel, alpha=alpha, tm=tm, mixer_tail=mixer is not None),
        grid=(bsz, t_len // tm),
        in_specs=in_specs,
        out_specs=(_row_spec(tm, d), _tail_spec(d_ff)),
        out_shape=(jax.ShapeDtypeStruct((bsz, t_len, d), F32),
                   jax.ShapeDtypeStruct((bsz, HALO, d_ff), F32)),
        scratch_shapes=[pltpu.VMEM((HALO, d_ff), F32), pltpu.VMEM((tm, d_ff), BF16)],
        compiler_params=_params(),
        name="conv_ffn",
    )(*operands)


def _gdn_mixer(hm, hx, w_in_f32, w_in, layer, conv_w, a_log, dt_bias, norm_w):
    w3 = 3 * GDN_WIDTH
    wba = jnp.pad(w_in_f32[:, w3 + GDN_WIDTH:], ((0, 0), (0, LANES - 2 * GDN_HEADS))).astype(BF16)
    lane_pad = (GDN_HEADS, LANES - 2 * GDN_HEADS)
    gscale = jnp.pad(-jnp.exp(a_log.astype(F32)), lane_pad)[None]
    dtb = jnp.pad(dt_bias.astype(F32), lane_pad)[None]
    nw = norm_w.astype(F32)[None]
    state_shape = (GDN_HEADS, GDN_HEAD_DIM, GDN_HEAD_DIM)

    act_m, tail_m = _gdn_in(hm, w_in, layer, wba, conv_w, gscale, dtb,
                            jnp.zeros((HALO, w3), F32), N_META)
    front = ((0, 0), (CHUNK - N_META, 0), (0, 0))
    om, state_m = _gdn(jnp.pad(act_m, front), nw, jnp.zeros(state_shape, F32), CHUNK)

    act_x, _ = _gdn_in(hx, w_in, layer, wba, conv_w, gscale, dtb, tail_m[0], ROW_TILE)
    ox, _ = _gdn(act_x, nw, state_m[0], GDN_ROW_TILE)
    return om[:, CHUNK - N_META:], ox


def _sconv_layer(hm, hx, w_in, w_out, layer, conv_w, ln_g, ln_b, alpha):
    g2, b2 = ln_g[None], ln_b[None]
    width = conv_w.shape[1]
    hm_new, tail_m = _sconv(hm, w_in, w_out, layer, conv_w, g2, b2,
                            jnp.zeros((HALO, width), F32), alpha, N_META)
    hx_new, _ = _sconv(hx, w_in, w_out, layer, conv_w, g2, b2, tail_m[0], alpha, ROW_TILE)
    return hm_new, hx_new


def _ffn_layer(hm, hx, w_up, w_down, layer, conv_w, ln_g, ln_b, alpha, mixer=None):
    d_ff = conv_w.shape[1]
    g2, b2 = ln_g[None], ln_b[None]
    mix_m = mix_x = None
    if mixer is not None:
        am, ax, w_o, w_o_layer, g1, b1 = mixer
        tail = (w_o, w_o_layer, g1[None], b1[None])
        mix_m, mix_x = (am,) + tail, (ax,) + tail
    hm_new, tail_m = _ffn(hm, w_up, w_down, layer, conv_w, g2, b2, jnp.zeros((HALO, d_ff), F32),
                          alpha, N_META, mix_m)
    hx_new, _ = _ffn(hx, w_up, w_down, layer, conv_w, g2, b2, tail_m[0], alpha, ROW_TILE, mix_x)
    return hm_new, hx_new


def kernel(x, meta, a_w_in, a_conv, a_log, a_dt_bias, a_norm, a_w_out, b_w_in, b_conv, b_w_out,
           ln_mix_g, ln_mix_b, ffn_w_up, ffn_conv, ffn_w_down, ln_ffn_g, ln_ffn_b):
    depth = ln_mix_g.shape[0]
    alpha = (2.0 * depth) ** 0.25
    assert x.shape[1] % ROW_TILE == 0 and meta.shape[0] == N_META
    hx = x
    hm = meta.astype(x.dtype)[None]
    a_in, a_out = a_w_in[:, :, :Z_COL + GDN_WIDTH].astype(BF16), a_w_out.astype(BF16)
    b_in, b_out = b_w_in.astype(BF16), b_w_out.astype(BF16)
    f_up, f_down = ffn_w_up.astype(BF16), ffn_w_down.astype(BF16)
    for i in range(depth):
        j = i // 2
        mixer = None
        if i % 2 == 0:
            am, ax = _gdn_mixer(hm, hx, a_w_in[j], a_in, j, a_conv[j], a_log[j], a_dt_bias[j],
                                a_norm[j])
            mixer = (am, ax, a_out, j, ln_mix_g[i], ln_mix_b[i])
        else:
            hm, hx = _sconv_layer(hm, hx, b_in, b_out, j, b_conv[j], ln_mix_g[i], ln_mix_b[i],
                                  alpha)
        hm, hx = _ffn_layer(hm, hx, f_up, f_down, i, ffn_conv[i], ln_ffn_g[i], ln_ffn_b[i], alpha,
                            mixer)
    return hx
```

```python
import functools

import jax
import jax.numpy as jnp
from jax import lax
from jax.experimental import pallas as pl
from jax.experimental.pallas import tpu as pltpu

F32 = jnp.float32
BF16 = jnp.bfloat16

N_META = 16
GDN_HEADS = 8
GDN_HEAD_DIM = 128
GDN_WIDTH = GDN_HEADS * GDN_HEAD_DIM
Q_SCALE = GDN_HEAD_DIM ** -0.5
CHUNK = 64
INV_BLOCK = 16
HALO = 8
LANES = 128
Q_COL, K_COL, V_COL, Z_COL, SC_COL = (i * GDN_WIDTH for i in range(5))
ACT_WIDTH = SC_COL + LANES
LN_EPS = 1e-5
RMS_EPS = 1e-6
L2_EPS = 1e-6
MASK_NEG = -1e30

ROW_TILE = 512
SUB_ROWS = 256
GDN_ROW_TILE = 512
GDN_GROUP = 4
QKV_COL_CHUNK = 512
EW_ROWS = 64
MXU_AHEAD = 2
QKV_SLOTS = MXU_AHEAD + 1
FFN_COL_CHUNK = 256
FFN_DOWN_GROUP = 4
SCONV_COL_CHUNK = 256
VMEM_LIMIT = 56 * 1024 * 1024


def _dot(a, b):
    return jnp.dot(a, b, preferred_element_type=F32)


def _dot_nt(a, b):
    return lax.dot_general(a, b, (((1,), (1,)), ((), ())), preferred_element_type=F32)


def _sigmoid(x):
    return 1.0 / (1.0 + jnp.exp(-x))


def _layer_norm(x, g, b):
    mu = jnp.mean(x, axis=-1, keepdims=True)
    xc = x - mu
    var = jnp.mean(xc * xc, axis=-1, keepdims=True)
    return xc * lax.rsqrt(var + LN_EPS) * g + b


def _const_spec(shape):
    nd = len(shape)
    return pl.BlockSpec(shape, lambda *_, _nd=nd: (0,) * _nd, pipeline_mode=pl.Buffered(1))


def _bf16_odd_pitch(w):
    tiles = -(-w.shape[-1] // LANES)
    pad = 0 if tiles % 2 else (tiles + 1) * LANES - w.shape[-1]
    return jnp.pad(w.astype(BF16), [(0, 0)] * (w.ndim - 1) + [(0, pad)])


def _layer_spec(stacked, layer):
    nd = stacked.ndim - 1
    return pl.BlockSpec((None,) + stacked.shape[1:], lambda *_, _nd=nd: (layer,) + (0,) * _nd,
                        pipeline_mode=pl.Buffered(1))


def _row_spec(tm, width):
    return pl.BlockSpec((1, tm, width), lambda b, t: (b, t, 0))


def _tail_spec(width):
    return pl.BlockSpec((1, HALO, width), lambda b, t: (b, 0, 0))


def _params():
    return pltpu.CompilerParams(dimension_semantics=("arbitrary", "arbitrary"),
                                vmem_limit_bytes=VMEM_LIMIT)


def _spread_taps(taps):
    return [jnp.broadcast_to(taps[j:j + 1, :], (HALO, taps.shape[1]))[None]
            for j in range(taps.shape[0])]


def _causal_taps(carry, cur, taps):
    width = len(taps)
    tm, c = cur.shape
    assert width - 1 <= HALO
    tiles = jnp.concatenate([carry, cur], axis=0).reshape(tm // HALO + 1, HALO, c)
    first = lax.broadcasted_iota(jnp.int32, (tm // HALO, HALO, c), 1) == 0
    acc = tiles * taps[0]
    for j in range(1, width):
        rot = pltpu.roll(acc, 1, 1)
        shifted = jnp.where(first, rot[:-1], rot[1:])
        if j < width - 1:
            acc = tiles * taps[j] + jnp.concatenate([rot[:1], shifted], axis=0)
        else:
            acc = tiles[1:] * taps[j] + shifted
    return acc.reshape(tm, c)


def _gdn_in_kernel(h_ref, w_ref, wba_ref, cw_ref, gs_ref, dtb_ref, halo_ref,
                   act_ref, tail_ref, st_ref, carry_ref, *, tm):
    t = pl.program_id(1)

    @pl.when(t == 0)
    def _():
        carry_ref[...] = halo_ref[...]

    hb = h_ref[0].astype(BF16)
    out_col = (Q_COL, K_COL, V_COL)
    heads_per_chunk = QKV_COL_CHUNK // GDN_HEAD_DIM
    chunks_per_out = GDN_WIDTH // QKV_COL_CHUNK
    n_chunks = 3 * chunks_per_out
    col_slice = lambda c: slice(c * QKV_COL_CHUNK, (c + 1) * QKV_COL_CHUNK)
    rb = min(EW_ROWS, tm)

    def project(c):
        st_ref[c % QKV_SLOTS, HALO:HALO + tm, :] = _dot(hb, w_ref[:, col_slice(c)])

    def project_z(i):
        zs = slice(Z_COL + i * QKV_COL_CHUNK, Z_COL + (i + 1) * QKV_COL_CHUNK)
        ws = slice(3 * GDN_WIDTH + i * QKV_COL_CHUNK, 3 * GDN_WIDTH + (i + 1) * QKV_COL_CHUNK)
        act_ref[0, :, zs] = _dot(hb, w_ref[:, ws])

    for c in range(min(MXU_AHEAD, n_chunks)):
        project(c)
    ba = None
    for c in range(n_chunks):
        cs = col_slice(c)
        st = st_ref.at[c % QKV_SLOTS]
        if c + MXU_AHEAD < n_chunks:
            project(c + MXU_AHEAD)
        if c < chunks_per_out:
            project_z(c)
        elif ba is None:
            ba = _dot(hb, wba_ref[...])
        st[0:HALO, :] = carry_ref[:, cs]
        carry_ref[:, cs] = st[tm:tm + HALO, :]
        which = c // chunks_per_out
        for hh in range(heads_per_chunk):
            hs = slice(hh * GDN_HEAD_DIM, (hh + 1) * GDN_HEAD_DIM)
            col = (c % chunks_per_out) * QKV_COL_CHUNK + hh * GDN_HEAD_DIM
            taps = _spread_taps(cw_ref[:, c * QKV_COL_CHUNK + hh * GDN_HEAD_DIM:
                                       c * QKV_COL_CHUNK + (hh + 1) * GDN_HEAD_DIM])
            for r0 in range(0, tm, rb):
                blk = st[r0:r0 + rb + HALO, hs]
                y = _causal_taps(blk[:HALO], blk[HALO:], taps)
                y = y * _sigmoid(y)
                if which < 2:
                    inv = lax.rsqrt(jnp.sum(y * y, axis=-1, keepdims=True) + L2_EPS)
                    y = y * (inv * Q_SCALE if which == 0 else inv)
                act_ref[0, r0:r0 + rb, out_col[which] + col:out_col[which] + col + GDN_HEAD_DIM] = y

    lane = lax.broadcasted_iota(jnp.int32, ba.shape, 1)
    xg = ba + dtb_ref[...]
    softplus = jnp.maximum(xg, 0.0) + jnp.log(1.0 + jnp.exp(-jnp.abs(xg)))
    act_ref[0, :, SC_COL:] = jnp.where(lane < GDN_HEADS, _sigmoid(ba), gs_ref[...] * softplus)

    @pl.when(t == pl.num_programs(1) - 1)
    def _():
        tail_ref[0] = carry_ref[...]


def _gdn_in(h, w_in, layer, wba, conv_w, gscale, dtb, halo, tm):
    bsz, t_len, d = h.shape
    w3 = 3 * GDN_WIDTH
    row = lambda width: _row_spec(tm, width)
    out_shape = (
        jax.ShapeDtypeStruct((bsz, t_len, ACT_WIDTH), F32),
        jax.ShapeDtypeStruct((bsz, HALO, w3), F32),
    )
    return pl.pallas_call(
        functools.partial(_gdn_in_kernel, tm=tm),
        grid=(bsz, t_len // tm),
        in_specs=[row(d), _layer_spec(w_in, layer), _const_spec(wba.shape),
                  _const_spec(conv_w.shape), _const_spec(gscale.shape), _const_spec(dtb.shape),
                  _const_spec(halo.shape)],
        out_specs=(row(ACT_WIDTH), _tail_spec(w3)),
        out_shape=out_shape,
        scratch_shapes=[pltpu.VMEM((QKV_SLOTS, tm + HALO, QKV_COL_CHUNK), F32),
                        pltpu.VMEM((HALO, w3), F32)],
        compiler_params=_params(),
        name="gdn_in",
    )(h, w_in, wba, conv_w, gscale, dtb, halo)


def _block_diag(x):
    half = x.shape[1] // 2
    zero = jnp.zeros((x.shape[0], half), x.dtype)
    return jnp.concatenate([jnp.concatenate([x[:, :half], zero], axis=1),
                            jnp.concatenate([zero, x[:, half:]], axis=1)], axis=0)


def _mm_pairs(a_list, b_list):
    return [_dot(a.astype(BF16), _block_diag(b.astype(BF16))) for a, b in zip(a_list, b_list)]


def _unit_lower_inverse(ms, eye, blockdiag, out):
    d = [jnp.where(blockdiag, m, 0.0) for m in ms]
    lo = [jnp.where(blockdiag, 0.0, m) for m in ms]
    d2 = _mm_pairs(d, d)
    yield
    d4, p = _square_and_times(d2, [eye - x for x in d], eye)
    yield
    d8, p = _square_and_times(d4, p, eye)
    yield
    p = _mm_pairs(p, [eye + x for x in d8])
    yield
    n = _mm_pairs(p, lo)
    yield
    n2 = _mm_pairs(n, n)
    yield
    qn = _mm_pairs([eye - x for x in n], [eye + x for x in n2])
    yield
    out[:] = _mm_pairs(qn, p)
    yield


def _square_and_times(xs, ps, eye):
    xb = [x.astype(BF16) for x in xs]
    both = _mm_pairs([jnp.concatenate([x, p.astype(BF16)], axis=0) for x, p in zip(xb, ps)],
                     [eye + x for x in xs])
    return ([r[:CHUNK] - x.astype(F32) for r, x in zip(both, xb)], [r[CHUNK:] for r in both])


def _interleave(major, minor, n_major, n_minor):
    done = 0
    for i, _ in enumerate(major):
        want = ((i + 1) * n_minor) // n_major
        while done < want and next(minor, StopIteration) is not StopIteration:
            done += 1
    for _ in minor:
        pass


PHASE1_STEPS = 11
PHASE2_STEPS_PER_CHUNK = 2


def _gdn_kernel(act_ref, nw_ref, s0_ref, o_ref, sfin_ref, state_ref,
                *, n_chunks):
    t = pl.program_id(1)

    @pl.when(t == 0)
    def _():
        state_ref[...] = s0_ref[...]

    dh = GDN_HEAD_DIM
    ri = lax.broadcasted_iota(jnp.int32, (CHUNK, 2 * CHUNK), 0)
    li = lax.broadcasted_iota(jnp.int32, (CHUNK, 2 * CHUNK), 1)
    left = li < CHUNK
    ci = jnp.where(left, li, li - CHUNK)
    causal = ri >= ci
    strict = ri > ci
    blockdiag = (ri // INV_BLOCK) == (ci // INV_BLOCK)
    eye = jnp.where(ri == ci, 1.0, 0.0).astype(F32)
    ri1 = lax.broadcasted_iota(jnp.int32, (CHUNK, CHUNK), 0)
    ci1 = lax.broadcasted_iota(jnp.int32, (CHUNK, CHUNK), 1)
    tril_b = jnp.where(ri1 >= ci1, 1.0, 0.0).astype(BF16)
    nw = nw_ref[...]

    def per_head(col_a, col_b):
        return jnp.concatenate([jnp.broadcast_to(col_a, (CHUNK, dh)),
                                jnp.broadcast_to(col_b, (CHUNK, dh))], axis=1)

    pairs = range(GDN_HEADS // 2)
    rows = [slice(c * CHUNK, (c + 1) * CHUNK) for c in range(n_chunks)]

    def col(x, lane):
        return x[:, lane:lane + 1]

    def phase1(chunks, res):
        items = [(c, j) for c in chunks for j in pairs]
        scc, gc, gct2 = {}, {}, {}
        for c in chunks:
            scc[c] = act_ref[0, rows[c], SC_COL:]
            hi = scc[c].astype(BF16)
            r1 = scc[c] - hi.astype(F32)
            mid = r1.astype(BF16)
            low = (r1 - mid.astype(F32)).astype(BF16)
            gc[c] = _dot(tril_b, hi) + _dot(tril_b, mid) + _dot(tril_b, low)
            gct2[c] = jnp.concatenate([gc[c], gc[c]], axis=0).T
        yield
        pair = lambda base, c, j: act_ref[0, rows[c], base + 2 * j * dh:base + (2 * j + 2) * dh]
        q = [pair(Q_COL, c, j) for c, j in items]
        k = [pair(K_COL, c, j) for c, j in items]
        v = [pair(V_COL, c, j) for c, j in items]
        g_a = [col(gc[c], GDN_HEADS + 2 * j) for c, j in items]
        g_b = [col(gc[c], GDN_HEADS + 2 * j + 1) for c, j in items]
        gl_a = [x[CHUNK - 1:CHUNK, :] for x in g_a]
        gl_b = [x[CHUNK - 1:CHUNK, :] for x in g_b]
        beta = [per_head(col(scc[c], 2 * j), col(scc[c], 2 * j + 1)) for c, j in items]
        eg = [per_head(jnp.exp(a), jnp.exp(b)) for a, b in zip(g_a, g_b)]
        kd_scale = [per_head(jnp.exp(la - a), jnp.exp(lb - b))
                    for a, b, la, lb in zip(g_a, g_b, gl_a, gl_b)]
        gcol = [jnp.where(left, a, b) for a, b in zip(g_a, g_b)]
        grow = [jnp.where(left[:1], gct2[c][GDN_HEADS + 2 * j:GDN_HEADS + 2 * j + 1, :],
                          gct2[c][GDN_HEADS + 2 * j + 1:GDN_HEADS + 2 * j + 2, :])
                for c, j in items]
        decay = [jnp.exp(jnp.where(causal, gc_ - gr_, MASK_NEG)) for gc_, gr_ in zip(gcol, grow)]
        kb = [k_ * b_ for k_, b_ in zip(k, beta)]
        kk_qk = [_dot_nt(jnp.concatenate([kb_, q_], axis=0).astype(BF16),
                         _block_diag(k_.astype(BF16))) for kb_, q_, k_ in zip(kb, q, k)]
        yield
        m = [jnp.where(strict, x[:CHUNK] * d_, 0.0) for x, d_ in zip(kk_qk, decay)]
        qk = [x[CHUNK:] * d_ for x, d_ in zip(kk_qk, decay)]
        tinv = []
        yield from _unit_lower_inverse(m, eye, blockdiag, tinv)
        vb = [v_ * b_ for v_, b_ in zip(v, beta)]
        kbg = [kb_ * e_ for kb_, e_ in zip(kb, eg)]
        uw = [_dot(_block_diag(t_.astype(BF16)),
                   jnp.concatenate([jnp.concatenate([vb_[:, :dh], kbg_[:, :dh]], axis=1),
                                    jnp.concatenate([vb_[:, dh:], kbg_[:, dh:]], axis=1)],
                                   axis=0).astype(BF16))
              for t_, vb_, kbg_ in zip(tinv, vb, kbg)]
        for i, item in enumerate(items):
            kd = k[i] * kd_scale[i]
            kdt = jnp.concatenate([kd[:, :dh], kd[:, dh:]], axis=0).T
            res[item] = dict(
                u=(uw[i][:CHUNK, :dh], uw[i][CHUNK:, :dh]),
                lhs1=jnp.concatenate([jnp.concatenate([uw[i][:CHUNK, dh:], uw[i][CHUNK:, dh:]],
                                                      axis=1), q[i] * eg[i]], axis=0).astype(BF16),
                lhs2=jnp.concatenate([qk[i], kdt], axis=0).astype(BF16),
                sdec=(jnp.exp(gl_a[i]), jnp.exp(gl_b[i])))
        yield

    def phase2(chunks, res):
        for c in chunks:
            r = [res[c, j] for j in pairs]
            s = [(state_ref[2 * j], state_ref[2 * j + 1]) for j in pairs]
            ws_qs = [_dot(x["lhs1"], _block_diag(jnp.concatenate(s_, axis=1).astype(BF16)))
                     for x, s_ in zip(r, s)]
            yield
            v_new = [jnp.concatenate([x["u"][0] - y[:CHUNK, :dh], x["u"][1] - y[:CHUNK, dh:]],
                                     axis=1) for x, y in zip(r, ws_qs)]
            r2 = [_dot(x["lhs2"], _block_diag(vn.astype(BF16))) for x, vn in zip(r, v_new)]
            for j in pairs:
                state_ref[2 * j] = s[j][0] * r[j]["sdec"][0] + r2[j][CHUNK:, :dh]
                state_ref[2 * j + 1] = s[j][1] * r[j]["sdec"][1] + r2[j][CHUNK:, dh:]
            yield
            for j in pairs:
                o2 = ws_qs[j][CHUNK:] + r2[j][:CHUNK]
                for half in range(2):
                    o = o2[:, half * dh:(half + 1) * dh]
                    o = o * lax.rsqrt(jnp.mean(o * o, axis=-1, keepdims=True) + RMS_EPS) * nw
                    cs = slice((2 * j + half) * dh, (2 * j + half + 1) * dh)
                    zh = act_ref[0, rows[c], Z_COL + cs.start:Z_COL + cs.stop]
                    o_ref[0, rows[c], cs] = (o * (zh * _sigmoid(zh))).astype(o_ref.dtype)

    groups = [list(range(g, min(g + GDN_GROUP, n_chunks))) for g in range(0, n_chunks, GDN_GROUP)]
    res = {}
    for _ in phase1(groups[0], res):
        pass
    for prev, cur in zip(groups[:-1], groups[1:]):
        _interleave(phase1(cur, res), phase2(prev, res), PHASE1_STEPS,
                    PHASE2_STEPS_PER_CHUNK * len(prev))
    for _ in phase2(groups[-1], res):
        pass

    @pl.when(t == pl.num_programs(1) - 1)
    def _():
        sfin_ref[0] = state_ref[...]


def _gdn(act, norm_w, state0, tm):
    bsz, t_len, _ = act.shape
    state_shape = (GDN_HEADS, GDN_HEAD_DIM, GDN_HEAD_DIM)
    return pl.pallas_call(
        functools.partial(_gdn_kernel, n_chunks=tm // CHUNK),
        grid=(bsz, t_len // tm),
        in_specs=[_row_spec(tm, ACT_WIDTH), _const_spec(norm_w.shape), _const_spec(state0.shape)],
        out_specs=(_row_spec(tm, GDN_WIDTH),
                   pl.BlockSpec((1,) + state_shape, lambda b, t: (b, 0, 0, 0))),
        out_shape=(jax.ShapeDtypeStruct((bsz, t_len, GDN_WIDTH), BF16),
                   jax.ShapeDtypeStruct((bsz,) + state_shape, F32)),
        scratch_shapes=[pltpu.VMEM(state_shape, F32)],
        compiler_params=_params(),
        name="gdn_scan",
    )(act, norm_w, state0)


def _sconv_kernel(h_ref, win_ref, cw_ref, wout_ref, g_ref, b_ref, halo_ref, o_ref, tail_ref,
                  carry_ref, act_ref, *, alpha, tm):
    t = pl.program_id(1)

    @pl.when(t == 0)
    def _():
        carry_ref[...] = halo_ref[...]

    width = cw_ref.shape[1]
    n_chunks = width // SCONV_COL_CHUNK
    rs = min(SUB_ROWS, tm)

    for r0 in range(0, tm, rs):
        rows = slice(r0, r0 + rs)
        hv = h_ref[0, rows, :]
        hb = hv.astype(BF16)

        def proj(c):
            lo = c * SCONV_COL_CHUNK
            return [_dot(hb, win_ref[:, part * width + lo:part * width + lo + SCONV_COL_CHUNK])
                    for part in range(3)]

        nxt = proj(0)
        for c in range(n_chunks):
            cs = slice(c * SCONV_COL_CHUNK, (c + 1) * SCONV_COL_CHUNK)
            b_gate, c_gate, xv = nxt
            if c + 1 < n_chunks:
                nxt = proj(c + 1)
            cx = c_gate * xv
            u = _causal_taps(carry_ref[:, cs], cx, _spread_taps(cw_ref[:, cs]))
            carry_ref[:, cs] = cx[rs - HALO:rs, :]
            act_ref[rows, cs] = (b_gate * u).astype(BF16)
        y = _dot(act_ref[rows, :], wout_ref[:, :h_ref.shape[2]])
        o_ref[0, rows, :] = _layer_norm(alpha * hv + y, g_ref[...], b_ref[...])

    @pl.when(t == pl.num_programs(1) - 1)
    def _():
        tail_ref[0] = carry_ref[...]


def _sconv(h, w_in, w_out, layer, conv_w, g, b, halo, alpha, tm):
    bsz, t_len, d = h.shape
    width = conv_w.shape[1]
    return pl.pallas_call(
        functools.partial(_sconv_kernel, alpha=alpha, tm=tm),
        grid=(bsz, t_len // tm),
        in_specs=[_row_spec(tm, d), _layer_spec(w_in, layer), _const_spec(conv_w.shape),
                  _layer_spec(w_out, layer), _const_spec(g.shape), _const_spec(b.shape),
                  _const_spec(halo.shape)],
        out_specs=(_row_spec(tm, d), _tail_spec(width)),
        out_shape=(jax.ShapeDtypeStruct((bsz, t_len, d), F32),
                   jax.ShapeDtypeStruct((bsz, HALO, width), F32)),
        scratch_shapes=[pltpu.VMEM((HALO, width), F32), pltpu.VMEM((tm, width), BF16)],
        compiler_params=_params(),
        name="sconv_mixer",
    )(h, w_in, conv_w, w_out, g, b, halo)


def _ffn_kernel(*refs, alpha, tm, mixer_tail):
    if mixer_tail:
        a_ref, wo_ref, g1_ref, b1_ref = refs[:4]
        refs = refs[4:]
    (h_ref, wup_ref, cw_ref, wd_ref, g_ref, b_ref, halo_ref, o_ref, tail_ref,
     carry_ref, act_ref) = refs
    t = pl.program_id(1)

    @pl.when(t == 0)
    def _():
        carry_ref[...] = halo_ref[...]

    d = h_ref.shape[2]
    d_ff = cw_ref.shape[1]
    n_chunks = d_ff // FFN_COL_CHUNK
    col_slice = lambda c: slice(c * FFN_COL_CHUNK, (c + 1) * FFN_COL_CHUNK)
    rs = min(SUB_ROWS, tm)

    subs = [slice(r0, r0 + rs) for r0 in range(0, tm, rs)]
    if mixer_tail:
        proj = [_dot(a_ref[0, rows, :], wo_ref[:, :d]) for rows in subs]
        for rows, y in zip(subs, proj):
            o_ref[0, rows, :] = _layer_norm(alpha * h_ref[0, rows, :] + y, g1_ref[...], b1_ref[...])
    x_ref = o_ref if mixer_tail else h_ref
    for rows in subs:
        hv = x_ref[0, rows, :]
        hb = hv.astype(BF16)
        up = lambda c: (_dot(hb, wup_ref[:, col_slice(c)]),
                        _dot(hb, wup_ref[:, d_ff + c * FFN_COL_CHUNK:d_ff + (c + 1) * FFN_COL_CHUNK]))
        nxt = up(0)
        y = None
        group_start = 0
        for c in range(n_chunks):
            cs = col_slice(c)
            u, gate = nxt
            if c + 1 < n_chunks:
                nxt = up(c + 1)
            uc = _causal_taps(carry_ref[:, cs], u, _spread_taps(cw_ref[:, cs]))
            carry_ref[:, cs] = u[rs - HALO:rs, :]
            act_ref[rows, cs] = (uc * _sigmoid(uc) * gate).astype(BF16)
            if (c + 1 - group_start) == FFN_DOWN_GROUP or c + 1 == n_chunks:
                ks = slice(group_start * FFN_COL_CHUNK, (c + 1) * FFN_COL_CHUNK)
                part = _dot(act_ref[rows, ks], wd_ref[ks, :d])
                y = part if y is None else y + part
                group_start = c + 1
        o_ref[0, rows, :] = _layer_norm(alpha * hv + y, g_ref[...], b_ref[...])

    @pl.when(t == pl.num_programs(1) - 1)
    def _():
        tail_ref[0] = carry_ref[...]


def _ffn(h, w_up, w_down, layer, conv_w, g, b, halo, alpha, tm, mixer=None):
    bsz, t_len, d = h.shape
    d_ff = conv_w.shape[1]
    operands = [h, w_up, conv_w, w_down, g, b, halo]
    in_specs = [_row_spec(tm, d), _layer_spec(w_up, layer), _const_spec(conv_w.shape),
                _layer_spec(w_down, layer)] + [_const_spec(x.shape) for x in (g, b, halo)]
    if mixer is not None:
        a, w_o, w_o_layer, g1, b1 = mixer
        operands = [a, w_o, g1, b1] + operands
        in_specs = [_row_spec(tm, a.shape[2]), _layer_spec(w_o, w_o_layer),
                    _const_spec(g1.shape), _const_spec(b1.shape)] + in_specs
    return pl.pallas_call(
        functools.partial(_ffn_kernel, alpha=alpha, tm=tm, mixer_tail=mixer is not None),
        grid=(bsz, t_len // tm),
        in_specs=in_specs,
        out_specs=(_row_spec(tm, d), _tail_spec(d_ff)),
        out_shape=(jax.ShapeDtypeStruct((bsz, t_len, d), F32),
                   jax.ShapeDtypeStruct((bsz, HALO, d_ff), F32)),
        scratch_shapes=[pltpu.VMEM((HALO, d_ff), F32), pltpu.VMEM((tm, d_ff), BF16)],
        compiler_params=_params(),
        name="conv_ffn",
    )(*operands)


def _gdn_mixer(hm, hx, w_in_f32, w_in, layer, conv_w, a_log, dt_bias, norm_w):
    w3 = 3 * GDN_WIDTH
    wba = jnp.pad(w_in_f32[:, w3 + GDN_WIDTH:], ((0, 0), (0, LANES - 2 * GDN_HEADS))).astype(BF16)
    lane_pad = (GDN_HEADS, LANES - 2 * GDN_HEADS)
    gscale = jnp.pad(-jnp.exp(a_log.astype(F32)), lane_pad)[None]
    dtb = jnp.pad(dt_bias.astype(F32), lane_pad)[None]
    nw = norm_w.astype(F32)[None]
    state_shape = (GDN_HEADS, GDN_HEAD_DIM, GDN_HEAD_DIM)

    act_m, tail_m = _gdn_in(hm, w_in, layer, wba, conv_w, gscale, dtb,
                            jnp.zeros((HALO, w3), F32), N_META)
    front = ((0, 0), (CHUNK - N_META, 0), (0, 0))
    om, state_m = _gdn(jnp.pad(act_m, front), nw, jnp.zeros(state_shape, F32), CHUNK)

    act_x, _ = _gdn_in(hx, w_in, layer, wba, conv_w, gscale, dtb, tail_m[0], ROW_TILE)
    ox, _ = _gdn(act_x, nw, state_m[0], GDN_ROW_TILE)
    return om[:, CHUNK - N_META:], ox


def _sconv_layer(hm, hx, w_in, w_out, layer, conv_w, ln_g, ln_b, alpha):
    g2, b2 = ln_g[None], ln_b[None]
    width = conv_w.shape[1]
    hm_new, tail_m = _sconv(hm, w_in, w_out, layer, conv_w, g2, b2,
                            jnp.zeros((HALO, width), F32), alpha, N_META)
    hx_new, _ = _sconv(hx, w_in, w_out, layer, conv_w, g2, b2, tail_m[0], alpha, ROW_TILE)
    return hm_new, hx_new


def _ffn_layer(hm, hx, w_up, w_down, layer, conv_w, ln_g, ln_b, alpha, mixer=None):
    d_ff = conv_w.shape[1]
    g2, b2 = ln_g[None], ln_b[None]
    mix_m = mix_x = None
    if mixer is not None:
        am, ax, w_o, w_o_layer, g1, b1 = mixer
        tail = (w_o, w_o_layer, g1[None], b1[None])
        mix_m, mix_x = (am,) + tail, (ax,) + tail
    hm_new, tail_m = _ffn(hm, w_up, w_down, layer, conv_w, g2, b2, jnp.zeros((HALO, d_ff), F32),
                          alpha, N_META, mix_m)
    hx_new, _ = _ffn(hx, w_up, w_down, layer, conv_w, g2, b2, tail_m[0], alpha, ROW_TILE, mix_x)
    return hm_new, hx_new


def kernel(x, meta, a_w_in, a_conv, a_log, a_dt_bias, a_norm, a_w_out, b_w_in, b_conv, b_w_out,
           ln_mix_g, ln_mix_b, ffn_w_up, ffn_conv, ffn_w_down, ln_ffn_g, ln_ffn_b):
    depth = ln_mix_g.shape[0]
    alpha = (2.0 * depth) ** 0.25
    assert x.shape[1] % ROW_TILE == 0 and meta.shape[0] == N_META
    hx = x
    hm = meta.astype(x.dtype)[None]
    a_in, a_out = _bf16_odd_pitch(a_w_in), _bf16_odd_pitch(a_w_out)
    b_in, b_out = _bf16_odd_pitch(b_w_in), _bf16_odd_pitch(b_w_out)
    f_up, f_down = _bf16_odd_pitch(ffn_w_up), _bf16_odd_pitch(ffn_w_down)
    for i in range(depth):
        j = i // 2
        mixer = None
        if i % 2 == 0:
            am, ax = _gdn_mixer(hm, hx, a_w_in[j], a_in, j, a_conv[j], a_log[j], a_dt_bias[j],
                                a_norm[j])
            mixer = (am, ax, a_out, j, ln_mix_g[i], ln_mix_b[i])
        else:
            hm, hx = _sconv_layer(hm, hx, b_in, b_out, j, b_conv[j], ln_mix_g[i], ln_mix_b[i],
                                  alpha)
        hm, hx = _ffn_layer(hm, hx, f_up, f_down, i, ffn_conv[i], ln_ffn_g[i], ln_ffn_b[i], alpha,
                            mixer)
    return hx
```

```python
import functools

import jax
import jax.numpy as jnp
from jax import lax
from jax.experimental import pallas as pl
from jax.experimental.pallas import tpu as pltpu

F32 = jnp.float32
BF16 = jnp.bfloat16

N_META = 16
GDN_HEADS = 8
GDN_HEAD_DIM = 128
GDN_WIDTH = GDN_HEADS * GDN_HEAD_DIM
Q_SCALE = GDN_HEAD_DIM ** -0.5
CHUNK = 64
INV_BLOCK = 16
HALO = 8
LANES = 128
Q_COL, K_COL, V_COL, Z_COL, SC_COL = (i * GDN_WIDTH for i in range(5))
ACT_WIDTH = SC_COL + LANES
LN_EPS = 1e-5
RMS_EPS = 1e-6
L2_EPS = 1e-6
MASK_NEG = -1e30

ROW_TILE = 512
SUB_ROWS = 256
GDN_ROW_TILE = 512
GDN_GROUP = 4
QKV_COL_CHUNK = 512
EW_ROWS = 64
MXU_AHEAD = 2
QKV_SLOTS = MXU_AHEAD + 1
FFN_COL_CHUNK = 256
FFN_DOWN_GROUP = 4
SCONV_COL_CHUNK = 256
VMEM_LIMIT = 56 * 1024 * 1024


def _dot(a, b):
    return jnp.dot(a, b, preferred_element_type=F32)


def _dot_nt(a, b):
    return lax.dot_general(a, b, (((1,), (1,)), ((), ())), preferred_element_type=F32)


def _sigmoid(x):
    return 1.0 / (1.0 + jnp.exp(-x))


def _layer_norm(x, g, b):
    mu = jnp.mean(x, axis=-1, keepdims=True)
    xc = x - mu
    var = jnp.mean(xc * xc, axis=-1, keepdims=True)
    return xc * lax.rsqrt(var + LN_EPS) * g + b


def _const_spec(shape):
    nd = len(shape)
    return pl.BlockSpec(shape, lambda *_, _nd=nd: (0,) * _nd, pipeline_mode=pl.Buffered(1))


def _bf16_odd_pitch(w):
    tiles = -(-w.shape[-1] // LANES)
    pad = 0 if tiles % 2 else (tiles + 1) * LANES - w.shape[-1]
    return jnp.pad(w.astype(BF16), [(0, 0)] * (w.ndim - 1) + [(0, pad)])


def _layer_spec(stacked, layer):
    nd = stacked.ndim - 1
    return pl.BlockSpec((None,) + stacked.shape[1:], lambda *_, _nd=nd: (layer,) + (0,) * _nd,
                        pipeline_mode=pl.Buffered(1))


def _row_spec(tm, width):
    return pl.BlockSpec((1, tm, width), lambda b, t: (b, t, 0))


def _tail_spec(width):
    return pl.BlockSpec((1, HALO, width), lambda b, t: (b, 0, 0))


def _params():
    return pltpu.CompilerParams(dimension_semantics=("arbitrary", "arbitrary"),
                                vmem_limit_bytes=VMEM_LIMIT)


def _spread_taps(taps):
    return [jnp.broadcast_to(taps[j:j + 1, :], (HALO, taps.shape[1]))[None]
            for j in range(taps.shape[0])]


def _causal_taps(carry, cur, taps):
    width = len(taps)
    tm, c = cur.shape
    assert width - 1 <= HALO
    tiles = jnp.concatenate([carry, cur], axis=0).reshape(tm // HALO + 1, HALO, c)
    first = lax.broadcasted_iota(jnp.int32, (tm // HALO, HALO, c), 1) == 0
    acc = tiles * taps[0]
    for j in range(1, width):
        rot = pltpu.roll(acc, 1, 1)
        shifted = jnp.where(first, rot[:-1], rot[1:])
        if j < width - 1:
            acc = tiles * taps[j] + jnp.concatenate([rot[:1], shifted], axis=0)
        else:
            acc = tiles[1:] * taps[j] + shifted
    return acc.reshape(tm, c)


def _gdn_in_kernel(h_ref, w_ref, wba_ref, cw_ref, gs_ref, dtb_ref, halo_ref,
                   act_ref, tail_ref, st_ref, carry_ref, *, tm):
    t = pl.program_id(1)

    @pl.when(t == 0)
    def _():
        carry_ref[...] = halo_ref[...]

    hb = h_ref[0].astype(BF16)
    out_col = (Q_COL, K_COL, V_COL)
    heads_per_chunk = QKV_COL_CHUNK // GDN_HEAD_DIM
    chunks_per_out = GDN_WIDTH // QKV_COL_CHUNK
    n_chunks = 3 * chunks_per_out
    col_slice = lambda c: slice(c * QKV_COL_CHUNK, (c + 1) * QKV_COL_CHUNK)
    rb = min(EW_ROWS, tm)

    def project(c):
        st_ref[c % QKV_SLOTS, HALO:HALO + tm, :] = _dot(hb, w_ref[:, col_slice(c)])

    def project_z(i):
        zs = slice(Z_COL + i * QKV_COL_CHUNK, Z_COL + (i + 1) * QKV_COL_CHUNK)
        ws = slice(3 * GDN_WIDTH + i * QKV_COL_CHUNK, 3 * GDN_WIDTH + (i + 1) * QKV_COL_CHUNK)
        act_ref[0, :, zs] = _dot(hb, w_ref[:, ws])

    for c in range(min(MXU_AHEAD, n_chunks)):
        project(c)
    ba = None
    for c in range(n_chunks):
        cs = col_slice(c)
        st = st_ref.at[c % QKV_SLOTS]
        if c + MXU_AHEAD < n_chunks:
            project(c + MXU_AHEAD)
        if c < chunks_per_out:
            project_z(c)
        elif ba is None:
            ba = _dot(hb, wba_ref[...])
        st[0:HALO, :] = carry_ref[:, cs]
        carry_ref[:, cs] = st[tm:tm + HALO, :]
        which = c // chunks_per_out
        for hh in range(heads_per_chunk):
            hs = slice(hh * GDN_HEAD_DIM, (hh + 1) * GDN_HEAD_DIM)
            col = (c % chunks_per_out) * QKV_COL_CHUNK + hh * GDN_HEAD_DIM
            taps = _spread_taps(cw_ref[:, c * QKV_COL_CHUNK + hh * GDN_HEAD_DIM:
                                       c * QKV_COL_CHUNK + (hh + 1) * GDN_HEAD_DIM])
            for r0 in range(0, tm, rb):
                blk = st[r0:r0 + rb + HALO, hs]
                y = _causal_taps(blk[:HALO], blk[HALO:], taps)
                y = y * _sigmoid(y)
                if which < 2:
                    inv = lax.rsqrt(jnp.sum(y * y, axis=-1, keepdims=True) + L2_EPS)
                    y = y * (inv * Q_SCALE if which == 0 else inv)
                act_ref[0, r0:r0 + rb, out_col[which] + col:out_col[which] + col + GDN_HEAD_DIM] = y

    lane = lax.broadcasted_iota(jnp.int32, ba.shape, 1)
    xg = ba + dtb_ref[...]
    softplus = jnp.maximum(xg, 0.0) + jnp.log(1.0 + jnp.exp(-jnp.abs(xg)))
    act_ref[0, :, SC_COL:] = jnp.where(lane < GDN_HEADS, _sigmoid(ba), gs_ref[...] * softplus)

    @pl.when(t == pl.num_programs(1) - 1)
    def _():
        tail_ref[0] = carry_ref[...]


def _gdn_in(h, w_in, layer, wba, conv_w, gscale, dtb, halo, tm):
    bsz, t_len, d = h.shape
    w3 = 3 * GDN_WIDTH
    row = lambda width: _row_spec(tm, width)
    out_shape = (
        jax.ShapeDtypeStruct((bsz, t_len, ACT_WIDTH), F32),
        jax.ShapeDtypeStruct((bsz, HALO, w3), F32),
    )
    return pl.pallas_call(
        functools.partial(_gdn_in_kernel, tm=tm),
        grid=(bsz, t_len // tm),
        in_specs=[row(d), _layer_spec(w_in, layer), _const_spec(wba.shape),
                  _const_spec(conv_w.shape), _const_spec(gscale.shape), _const_spec(dtb.shape),
                  _const_spec(halo.shape)],
        out_specs=(row(ACT_WIDTH), _tail_spec(w3)),
        out_shape=out_shape,
        scratch_shapes=[pltpu.VMEM((QKV_SLOTS, tm + HALO, QKV_COL_CHUNK), F32),
                        pltpu.VMEM((HALO, w3), F32)],
        compiler_params=_params(),
        name="gdn_in",
    )(h, w_in, wba, conv_w, gscale, dtb, halo)


def _block_diag(x):
    half = x.shape[1] // 2
    zero = jnp.zeros((x.shape[0], half), x.dtype)
    return jnp.concatenate([jnp.concatenate([x[:, :half], zero], axis=1),
                            jnp.concatenate([zero, x[:, half:]], axis=1)], axis=0)


def _mm_pairs(a_list, b_list):
    return [_dot(a.astype(BF16), _block_diag(b.astype(BF16))) for a, b in zip(a_list, b_list)]


def _unit_lower_inverse(ms, eye, blockdiag, out):
    d = [jnp.where(blockdiag, m, 0.0) for m in ms]
    lo = [jnp.where(blockdiag, 0.0, m) for m in ms]
    d2 = _mm_pairs(d, d)
    yield
    d4, p = _square_and_times(d2, [eye - x for x in d], eye)
    yield
    d8, p = _square_and_times(d4, p, eye)
    yield
    p = _mm_pairs(p, [eye + x for x in d8])
    yield
    n = _mm_pairs(p, lo)
    yield
    n2 = _mm_pairs(n, n)
    yield
    qn = _mm_pairs([eye - x for x in n], [eye + x for x in n2])
    yield
    out[:] = _mm_pairs(qn, p)
    yield


def _square_and_times(xs, ps, eye):
    xb = [x.astype(BF16) for x in xs]
    both = _mm_pairs([jnp.concatenate([x, p.astype(BF16)], axis=0) for x, p in zip(xb, ps)],
                     [eye + x for x in xs])
    return ([r[:CHUNK] - x.astype(F32) for r, x in zip(both, xb)], [r[CHUNK:] for r in both])


def _interleave(major, minor, n_major, n_minor):
    done = 0
    for i, _ in enumerate(major):
        want = ((i + 1) * n_minor) // n_major
        while done < want and next(minor, StopIteration) is not StopIteration:
            done += 1
    for _ in minor:
        pass


PHASE1_STEPS = 11
PHASE2_STEPS_PER_CHUNK = 2


def _gdn_kernel(act_ref, nw_ref, s0_ref, o_ref, sfin_ref, state_ref,
                *, n_chunks):
    t = pl.program_id(1)

    @pl.when(t == 0)
    def _():
        state_ref[...] = s0_ref[...]

    dh = GDN_HEAD_DIM
    ri = lax.broadcasted_iota(jnp.int32, (CHUNK, 2 * CHUNK), 0)
    li = lax.broadcasted_iota(jnp.int32, (CHUNK, 2 * CHUNK), 1)
    left = li < CHUNK
    ci = jnp.where(left, li, li - CHUNK)
    causal = ri >= ci
    strict = ri > ci
    blockdiag = (ri // INV_BLOCK) == (ci // INV_BLOCK)
    eye = jnp.where(ri == ci, 1.0, 0.0).astype(F32)
    ri1 = lax.broadcasted_iota(jnp.int32, (CHUNK, CHUNK), 0)
    ci1 = lax.broadcasted_iota(jnp.int32, (CHUNK, CHUNK), 1)
    tril_b = jnp.where(ri1 >= ci1, 1.0, 0.0).astype(BF16)
    nw = nw_ref[...]

    def per_head(col_a, col_b):
        return jnp.concatenate([jnp.broadcast_to(col_a, (CHUNK, dh)),
                                jnp.broadcast_to(col_b, (CHUNK, dh))], axis=1)

    pairs = range(GDN_HEADS // 2)
    rows = [slice(c * CHUNK, (c + 1) * CHUNK) for c in range(n_chunks)]

    def col(x, lane):
        return x[:, lane:lane + 1]

    def phase1(chunks, res):
        items = [(c, j) for c in chunks for j in pairs]
        scc, gc, gct2 = {}, {}, {}
        for c in chunks:
            scc[c] = act_ref[0, rows[c], SC_COL:]
            hi = scc[c].astype(BF16)
            r1 = scc[c] - hi.astype(F32)
            mid = r1.astype(BF16)
            low = (r1 - mid.astype(F32)).astype(BF16)
            gc[c] = _dot(tril_b, hi) + _dot(tril_b, mid) + _dot(tril_b, low)
            gct2[c] = jnp.concatenate([gc[c], gc[c]], axis=0).T
        yield
        pair = lambda base, c, j: act_ref[0, rows[c], base + 2 * j * dh:base + (2 * j + 2) * dh]
        q = [pair(Q_COL, c, j) for c, j in items]
        k = [pair(K_COL, c, j) for c, j in items]
        v = [pair(V_COL, c, j) for c, j in items]
        g_a = [col(gc[c], GDN_HEADS + 2 * j) for c, j in items]
        g_b = [col(gc[c], GDN_HEADS + 2 * j + 1) for c, j in items]
        gl_a = [x[CHUNK - 1:CHUNK, :] for x in g_a]
        gl_b = [x[CHUNK - 1:CHUNK, :] for x in g_b]
        beta = [per_head(col(scc[c], 2 * j), col(scc[c], 2 * j + 1)) for c, j in items]
        eg = [per_head(jnp.exp(a), jnp.exp(b)) for a, b in zip(g_a, g_b)]
        kd_scale = [per_head(jnp.exp(la - a), jnp.exp(lb - b))
                    for a, b, la, lb in zip(g_a, g_b, gl_a, gl_b)]
        gcol = [jnp.where(left, a, b) for a, b in zip(g_a, g_b)]
        grow = [jnp.where(left[:1], gct2[c][GDN_HEADS + 2 * j:GDN_HEADS + 2 * j + 1, :],
                          gct2[c][GDN_HEADS + 2 * j + 1:GDN_HEADS + 2 * j + 2, :])
                for c, j in items]
        decay = [jnp.exp(jnp.where(causal, gc_ - gr_, MASK_NEG)) for gc_, gr_ in zip(gcol, grow)]
        kb = [k_ * b_ for k_, b_ in zip(k, beta)]
        kk_qk = [_dot_nt(jnp.concatenate([kb_, q_], axis=0).astype(BF16),
                         _block_diag(k_.astype(BF16))) for kb_, q_, k_ in zip(kb, q, k)]
        yield
        m = [jnp.where(strict, x[:CHUNK] * d_, 0.0) for x, d_ in zip(kk_qk, decay)]
        qk = [x[CHUNK:] * d_ for x, d_ in zip(kk_qk, decay)]
        tinv = []
        yield from _unit_lower_inverse(m, eye, blockdiag, tinv)
        vb = [v_ * b_ for v_, b_ in zip(v, beta)]
        kbg = [kb_ * e_ for kb_, e_ in zip(kb, eg)]
        uw = [_dot(_block_diag(t_.astype(BF16)),
                   jnp.concatenate([jnp.concatenate([vb_[:, :dh], kbg_[:, :dh]], axis=1),
                                    jnp.concatenate([vb_[:, dh:], kbg_[:, dh:]], axis=1)],
                                   axis=0).astype(BF16))
              for t_, vb_, kbg_ in zip(tinv, vb, kbg)]
        for i, item in enumerate(items):
            kd = k[i] * kd_scale[i]
            kdt = jnp.concatenate([kd[:, :dh], kd[:, dh:]], axis=0).T
            res[item] = dict(
                u=(uw[i][:CHUNK, :dh], uw[i][CHUNK:, :dh]),
                lhs1=jnp.concatenate([jnp.concatenate([uw[i][:CHUNK, dh:], uw[i][CHUNK:, dh:]],
                                                      axis=1), q[i] * eg[i]], axis=0).astype(BF16),
                lhs2=jnp.concatenate([qk[i], kdt], axis=0).astype(BF16),
                sdec=(jnp.exp(gl_a[i]), jnp.exp(gl_b[i])))
        yield

    def phase2(chunks, res):
        for c in chunks:
            r = [res[c, j] for j in pairs]
            s = [(state_ref[2 * j], state_ref[2 * j + 1]) for j in pairs]
            ws_qs = [_dot(x["lhs1"], _block_diag(jnp.concatenate(s_, axis=1).astype(BF16)))
                     for x, s_ in zip(r, s)]
            yield
            v_new = [jnp.concatenate([x["u"][0] - y[:CHUNK, :dh], x["u"][1] - y[:CHUNK, dh:]],
                                     axis=1) for x, y in zip(r, ws_qs)]
            r2 = [_dot(x["lhs2"], _block_diag(vn.astype(BF16))) for x, vn in zip(r, v_new)]
            for j in pairs:
                state_ref[2 * j] = s[j][0] * r[j]["sdec"][0] + r2[j][CHUNK:, :dh]
                state_ref[2 * j + 1] = s[j][1] * r[j]["sdec"][1] + r2[j][CHUNK:, dh:]
            yield
            for j in pairs:
                o2 = ws_qs[j][CHUNK:] + r2[j][:CHUNK]
                for half in range(2):
                    o = o2[:, half * dh:(half + 1) * dh]
                    o = o * lax.rsqrt(jnp.mean(o * o, axis=-1, keepdims=True) + RMS_EPS) * nw
                    cs = slice((2 * j + half) * dh, (2 * j + half + 1) * dh)
                    zh = act_ref[0, rows[c], Z_COL + cs.start:Z_COL + cs.stop]
                    o_ref[0, rows[c], cs] = (o * (zh * _sigmoid(zh))).astype(o_ref.dtype)

    groups = [list(range(g, min(g + GDN_GROUP, n_chunks))) for g in range(0, n_chunks, GDN_GROUP)]
    res = {}
    for _ in phase1(groups[0], res):
        pass
    for prev, cur in zip(groups[:-1], groups[1:]):
        _interleave(phase1(cur, res), phase2(prev, res), PHASE1_STEPS,
                    PHASE2_STEPS_PER_CHUNK * len(prev))
    for _ in phase2(groups[-1], res):
        pass

    @pl.when(t == pl.num_programs(1) - 1)
    def _():
        sfin_ref[0] = state_ref[...]


def _gdn(act, norm_w, state0, tm):
    bsz, t_len, _ = act.shape
    state_shape = (GDN_HEADS, GDN_HEAD_DIM, GDN_HEAD_DIM)
    return pl.pallas_call(
        functools.partial(_gdn_kernel, n_chunks=tm // CHUNK),
        grid=(bsz, t_len // tm),
        in_specs=[_row_spec(tm, ACT_WIDTH), _const_spec(norm_w.shape), _const_spec(state0.shape)],
        out_specs=(_row_spec(tm, GDN_WIDTH),
                   pl.BlockSpec((1,) + state_shape, lambda b, t: (b, 0, 0, 0))),
        out_shape=(jax.ShapeDtypeStruct((bsz, t_len, GDN_WIDTH), BF16),
                   jax.ShapeDtypeStruct((bsz,) + state_shape, F32)),
        scratch_shapes=[pltpu.VMEM(state_shape, F32)],
        compiler_params=_params(),
        name="gdn_scan",
    )(act, norm_w, state0)


def _sconv_kernel(h_ref, win_ref, cw_ref, wout_ref, g_ref, b_ref, halo_ref, o_ref, tail_ref,
                  carry_ref, act_ref, *, alpha, tm):
    t = pl.program_id(1)

    @pl.when(t == 0)
    def _():
        carry_ref[...] = halo_ref[...]

    width = cw_ref.shape[1]
    n_chunks = width // SCONV_COL_CHUNK
    rs = min(SUB_ROWS, tm)

    for r0 in range(0, tm, rs):
        rows = slice(r0, r0 + rs)
        hv = h_ref[0, rows, :]
        hb = hv.astype(BF16)

        def proj(c):
            lo = c * SCONV_COL_CHUNK
            return [_dot(hb, win_ref[:, part * width + lo:part * width + lo + SCONV_COL_CHUNK])
                    for part in range(3)]

        nxt = proj(0)
        for c in range(n_chunks):
            cs = slice(c * SCONV_COL_CHUNK, (c + 1) * SCONV_COL_CHUNK)
            b_gate, c_gate, xv = nxt
            if c + 1 < n_chunks:
                nxt = proj(c + 1)
            cx = c_gate * xv
            u = _causal_taps(carry_ref[:, cs], cx, _spread_taps(cw_ref[:, cs]))
            carry_ref[:, cs] = cx[rs - HALO:rs, :]
            act_ref[rows, cs] = (b_gate * u).astype(BF16)
        y = _dot(act_ref[rows, :], wout_ref[:, :h_ref.shape[2]])
        o_ref[0, rows, :] = _layer_norm(alpha * hv + y, g_ref[...], b_ref[...])

    @pl.when(t == pl.num_programs(1) - 1)
    def _():
        tail_ref[0] = carry_ref[...]


def _sconv(h, w_in, w_out, layer, conv_w, g, b, halo, alpha, tm):
    bsz, t_len, d = h.shape
    width = conv_w.shape[1]
    return pl.pallas_call(
        functools.partial(_sconv_kernel, alpha=alpha, tm=tm),
        grid=(bsz, t_len // tm),
        in_specs=[_row_spec(tm, d), _layer_spec(w_in, layer), _const_spec(conv_w.shape),
                  _layer_spec(w_out, layer), _const_spec(g.shape), _const_spec(b.shape),
                  _const_spec(halo.shape)],
        out_specs=(_row_spec(tm, d), _tail_spec(width)),
        out_shape=(jax.ShapeDtypeStruct((bsz, t_len, d), F32),
                   jax.ShapeDtypeStruct((bsz, HALO, width), F32)),
        scratch_shapes=[pltpu.VMEM((HALO, width), F32), pltpu.VMEM((tm, width), BF16)],
        compiler_params=_params(),
        name="sconv_mixer",
    )(h, w_in, conv_w, w_out, g, b, halo)


def _ffn_kernel(*refs, alpha, tm, mixer_tail):
    if mixer_tail:
        a_ref, wo_ref, g1_ref, b1_ref = refs[:4]
        refs = refs[4:]
    (h_ref, wup_ref, cw_ref, wd_ref, g_ref, b_ref, halo_ref, o_ref, tail_ref,
     carry_ref, act_ref) = refs
    t = pl.program_id(1)

    @pl.when(t == 0)
    def _():
        carry_ref[...] = halo_ref[...]

    d = h_ref.shape[2]
    d_ff = cw_ref.shape[1]
    n_chunks = d_ff // FFN_COL_CHUNK
    col_slice = lambda c: slice(c * FFN_COL_CHUNK, (c + 1) * FFN_COL_CHUNK)
    rs = min(SUB_ROWS, tm)

    subs = [slice(r0, r0 + rs) for r0 in range(0, tm, rs)]
    if mixer_tail:
        proj = [_dot(a_ref[0, rows, :], wo_ref[:, :d]) for rows in subs]
        for rows, y in zip(subs, proj):
            o_ref[0, rows, :] = _layer_norm(alpha * h_ref[0, rows, :] + y, g1_ref[...], b1_ref[...])
    x_ref = o_ref if mixer_tail else h_ref
    for rows in subs:
        hv = x_ref[0, rows, :]
        hb = hv.astype(BF16)
        up = lambda c: (_dot(hb, wup_ref[:, col_slice(c)]),
                        _dot(hb, wup_ref[:, d_ff + c * FFN_COL_CHUNK:d_ff + (c + 1) * FFN_COL_CHUNK]))
        nxt = up(0)
        y = None
        group_start = 0
        for c in range(n_chunks):
            cs = col_slice(c)
            u, gate = nxt
            if c + 1 < n_chunks:
                nxt = up(c + 1)
            uc = _causal_taps(carry_ref[:, cs], u, _spread_taps(cw_ref[:, cs]))
            carry_ref[:, cs] = u[rs - HALO:rs, :]
            act_ref[rows, cs] = (uc * _sigmoid(uc) * gate).astype(BF16)
            if (c + 1 - group_start) == FFN_DOWN_GROUP or c + 1 == n_chunks:
                ks = slice(group_start * FFN_COL_CHUNK, (c + 1) * FFN_COL_CHUNK)
                part = _dot(act_ref[rows, ks], wd_ref[ks, :d])
                y = part if y is None else y + part
                group_start = c + 1
        o_ref[0, rows, :] = _layer_norm(alpha * hv + y, g_ref[...], b_ref[...])

    @pl.when(t == pl.num_programs(1) - 1)
    def _():
        tail_ref[0] = carry_ref[...]


def _ffn(h, w_up, w_down, layer, conv_w, g, b, halo, alpha, tm, mixer=None):
    bsz, t_len, d = h.shape
    d_ff = conv_w.shape[1]
    operands = [h, w_up, conv_w, w_down, g, b, halo]
    in_specs = [_row_spec(tm, d), _layer_spec(w_up, layer), _const_spec(conv_w.shape),
                _layer_spec(w_down, layer)] + [_const_spec(x.shape) for x in (g, b, halo)]
    if mixer is not None:
        a, w_o, w_o_layer, g1, b1 = mixer
        operands = [a, w_o, g1, b1] + operands
        in_specs = [_row_spec(tm, a.shape[2]), _layer_spec(w_o, w_o_layer),
                    _const_spec(g1.shape), _const_spec(b1.shape)] + in_specs
    return pl.pallas_call(
        functools.partial(_ffn_kernel, alpha=alpha, tm=tm, mixer_tail=mixer is not None),
        grid=(bsz, t_len // tm),
        in_specs=in_specs,
        out_specs=(_row_spec(tm, d), _tail_spec(d_ff)),
        out_shape=(jax.ShapeDtypeStruct((bsz, t_len, d), F32),
                   jax.ShapeDtypeStruct((bsz, HALO, d_ff), F32)),
        scratch_shapes=[pltpu.VMEM((HALO, d_ff), F32), pltpu.VMEM((tm, d_ff), BF16)],
        compiler_params=_params(),
        name="conv_ffn",
    )(*operands)


def _gdn_mixer(hm, hx, w_in_f32, w_in, layer, conv_w, a_log, dt_bias, norm_w):
    w3 = 3 * GDN_WIDTH
    wba = jnp.pad(w_in_f32[:, w3 + GDN_WIDTH:], ((0, 0), (0, LANES - 2 * GDN_HEADS))).astype(BF16)
    lane_pad = (GDN_HEADS, LANES - 2 * GDN_HEADS)
    gscale = jnp.pad(-jnp.exp(a_log.astype(F32)), lane_pad)[None]
    dtb = jnp.pad(dt_bias.astype(F32), lane_pad)[None]
    nw = norm_w.astype(F32)[None]
    state_shape = (GDN_HEADS, GDN_HEAD_DIM, GDN_HEAD_DIM)

    act_m, tail_m = _gdn_in(hm, w_in, layer, wba, conv_w, gscale, dtb,
                            jnp.zeros((HALO, w3), F32), N_META)
    front = ((0, 0), (CHUNK - N_META, 0), (0, 0))
    om, state_m = _gdn(jnp.pad(act_m, front), nw, jnp.zeros(state_shape, F32), CHUNK)

    act_x, _ = _gdn_in(hx, w_in, layer, wba, conv_w, gscale, dtb, tail_m[0], ROW_TILE)
    ox, _ = _gdn(act_x, nw, state_m[0], GDN_ROW_TILE)
    return om[:, CHUNK - N_META:], ox


def _sconv_layer(hm, hx, w_in, w_out, layer, conv_w, ln_g, ln_b, alpha):
    g2, b2 = ln_g[None], ln_b[None]
    width = conv_w.shape[1]
    hm_new, tail_m = _sconv(hm, w_in, w_out, layer, conv_w, g2, b2,
                            jnp.zeros((HALO, width), F32), alpha, N_META)
    hx_new, _ = _sconv(hx, w_in, w_out, layer, conv_w, g2, b2, tail_m[0], alpha, ROW_TILE)
    return hm_new, hx_new


def _ffn_layer(hm, hx, w_up, w_down, layer, conv_w, ln_g, ln_b, alpha, mixer=None):
    d_ff = conv_w.shape[1]
    g2, b2 = ln_g[None], ln_b[None]
    mix_m = mix_x = None
    if mixer is not None:
        am, ax, w_o, w_o_layer, g1, b1 = mixer
        tail = (w_o, w_o_layer, g1[None], b1[None])
        mix_m, mix_x = (am,) + tail, (ax,) + tail
    hm_new, tail_m = _ffn(hm, w_up, w_down, layer, conv_w, g2, b2, jnp.zeros((HALO, d_ff), F32),
                          alpha, N_META, mix_m)
    hx_new, _ = _ffn(hx, w_up, w_down, layer, conv_w, g2, b2, tail_m[0], alpha, ROW_TILE, mix_x)
    return hm_new, hx_new


def kernel(x, meta, a_w_in, a_conv, a_log, a_dt_bias, a_norm, a_w_out, b_w_in, b_conv, b_w_out,
           ln_mix_g, ln_mix_b, ffn_w_up, ffn_conv, ffn_w_down, ln_ffn_g, ln_ffn_b):
    depth = ln_mix_g.shape[0]
    alpha = (2.0 * depth) ** 0.25
    assert x.shape[1] % ROW_TILE == 0 and meta.shape[0] == N_META
    hx = x
    hm = meta.astype(x.dtype)[None]
    a_in, a_out = _bf16_odd_pitch(a_w_in), _bf16_odd_pitch(a_w_out)
    b_in, b_out = _bf16_odd_pitch(b_w_in), _bf16_odd_pitch(b_w_out)
    f_up, f_down = ffn_w_up.astype(BF16), ffn_w_down.astype(BF16)
    for i in range(depth):
        j = i // 2
        mixer = None
        if i % 2 == 0:
            am, ax = _gdn_mixer(hm, hx, a_w_in[j], a_in, j, a_conv[j], a_log[j], a_dt_bias[j],
                                a_norm[j])
            mixer = (am, ax, a_out, j, ln_mix_g[i], ln_mix_b[i])
        else:
            hm, hx = _sconv_layer(hm, hx, b_in, b_out, j, b_conv[j], ln_mix_g[i], ln_mix_b[i],
                                  alpha)
        hm, hx = _ffn_layer(hm, hx, f_up, f_down, i, ffn_conv[i], ln_ffn_g[i], ln_ffn_b[i], alpha,
                            mixer)
    return hx
```
